```python
import math
import jax, jax.numpy as jnp
from jax import lax
import numpy as np

D_MODEL = 1024
BATCH = 16
SEQ = 2048
DEPTH = 2
DEC_BATCH = 32
DEC_SEQ = 1
PAST_LEN = 16384
PAGE_SIZE = 128

HEAD_DIM = 64
MIX_W = D_MODEL
GROUP_W = MIX_W // 4
N_NSA_HEADS = GROUP_W // HEAD_DIM
N_NSA_KV = 1
NSA_G = N_NSA_HEADS // N_NSA_KV
L_CMP = 32
L_SLC = 64
N_SEL = 16
WINDOW = 512
CMP_HID = 4 * HEAD_DIM
Q_BLOCK = 64
N_HG_HEADS = GROUP_W // HEAD_DIM
HG_DK = HEAD_DIM
HG_DV = HEAD_DIM
HG_CHUNK = 64
POOL_SIZES = (2, 4, 8, 16)
POOL_CH = GROUP_W // len(POOL_SIZES)
POOL_BUF = max(POOL_SIZES) - 1
N_MEM = 256
N_MEM_HEADS = 4
MEM_HD = GROUP_W // N_MEM_HEADS
ROPE_THETA = 10000.0
EPS = 1e-6
IN_WIDTHS = (N_NSA_HEADS * HEAD_DIM, 6 * N_NSA_KV * HEAD_DIM, 3 * N_NSA_HEADS,
             GROUP_W, GROUP_W, GROUP_W, GROUP_W, GROUP_W, MIX_W)
IN_W = sum(IN_WIDTHS)

kernel_name = 'hybrid_nsa_hgrn2_pool_memory_decode_step'


def rms_norm(x, g):
    xf = x.astype(jnp.float32)
    y = xf * lax.rsqrt(jnp.mean(xf * xf, axis=-1, keepdims=True) + EPS)
    return (y * g.astype(jnp.float32)).astype(x.dtype)


def rope(x, pos):
    half = x.shape[-1] // 2
    inv = ROPE_THETA ** (-jnp.arange(half, dtype=jnp.float32) / half)
    ang = pos.astype(jnp.float32)[:, None] * inv[None, :]
    cos = jnp.cos(ang)[None, :, None, :]
    sin = jnp.sin(ang)[None, :, None, :]
    xf = x.astype(jnp.float32)
    x1, x2 = xf[..., :half], xf[..., half:]
    return jnp.concatenate([x1 * cos - x2 * sin, x2 * cos + x1 * sin], axis=-1).astype(x.dtype)


def masked_softmax(s, mask, axis):
    s = jnp.where(mask, s.astype(jnp.float32), -jnp.inf)
    m = jnp.max(s, axis=axis, keepdims=True)
    m = jnp.where(jnp.isfinite(m), m, 0.0)
    p = jnp.exp(s - m)
    return p / jnp.maximum(jnp.sum(p, axis=axis, keepdims=True), 1e-30)


def compress_rows(rows, pe, w1, w2):
    B, L, Hk, d = rows.shape
    n = L // L_CMP
    blk = rows[:, :n * L_CMP].reshape(B, n, L_CMP, Hk, d) + pe[:, None, :].astype(rows.dtype)
    flat = blk.transpose(0, 1, 3, 2, 4).reshape(B, n, Hk, L_CMP * d)
    return jax.nn.silu(flat @ w1) @ w2


def nsa_mix(q, kc, vc, ks, vs, kw, vw, gates, q_pos0, kw_pos0, cmp_pe, cmp_w1, cmp_w2):
    B, T, Hk, G, d = q.shape
    L = kc.shape[1]
    scale = d ** -0.5
    tpos = q_pos0 + jnp.arange(T)
    kcc = compress_rows(kc, cmp_pe[0], cmp_w1[0], cmp_w2[0])
    vcc = compress_rows(vc, cmp_pe[1], cmp_w1[1], cmp_w2[1])
    n_cmp = kcc.shape[1]
    done = (jnp.arange(n_cmp) + 1) * L_CMP - 1 <= tpos[:, None]
    s_c = jnp.einsum('bthgd,bnhd->bthgn', q, kcc) * scale
    p_c = masked_softmax(s_c, done[None, :, None, None, :], -1)
    o_c = jnp.einsum('bthgn,bnhd->bthgd', p_c.astype(vcc.dtype), vcc)
    ratio = L_SLC // L_CMP
    n_slc = -(-L // L_SLC)
    imp = jnp.sum(p_c, axis=3)
    imp = jnp.pad(imp, ((0, 0), (0, 0), (0, 0), (0, ratio * n_slc - n_cmp)))
    imp = imp.reshape(B, T, Hk, n_slc, ratio).sum(-1)
    blk_t = (tpos // L_SLC)[:, None]
    j = jnp.arange(n_slc)[None, :]
    forced = (j == 0) | (j == blk_t) | (j == blk_t - 1)
    avail = j <= blk_t
    score = jnp.where(avail[None, :, None, :],
                      jnp.where(forced[None, :, None, :], jnp.inf, imp), -jnp.inf)
    k_sel = min(N_SEL, n_slc)
    _, idx = lax.top_k(score, k_sel)
    sel_ok = idx <= blk_t[None, :, :, None]
    pad = n_slc * L_SLC - L

    def to_sel_blocks(a):
        a = jnp.pad(a, ((0, 0), (0, pad), (0, 0), (0, 0)))
        return a.reshape(B, n_slc, L_SLC, Hk, d).transpose(0, 3, 1, 2, 4)

    ks_b, vs_b = to_sel_blocks(ks), to_sel_blocks(vs)
    kw_p = jnp.pad(kw, ((0, 0), (WINDOW, 0), (0, 0), (0, 0)))
    vw_p = jnp.pad(vw, ((0, 0), (WINDOW, 0), (0, 0), (0, 0)))
    qb = Q_BLOCK if T % Q_BLOCK == 0 else T
    nq = T // qb

    def split_q(a):
        return jnp.moveaxis(a.reshape((B, nq, qb) + a.shape[2:]), 1, 0)

    gather_blocks = jax.vmap(jax.vmap(lambda blocks, ix: blocks[ix]))

    def block_fn(args):
        qq, ii, ok, i0 = args
        tq = q_pos0 + i0 + jnp.arange(qb)
        ix = ii.transpose(0, 2, 1, 3)
        kg = gather_blocks(ks_b, ix)
        vg = gather_blocks(vs_b, ix)
        s_s = jnp.einsum('bthgd,bhtkpd->bthgkp', qq, kg) * scale
        kpos = ii[..., None] * L_SLC + jnp.arange(L_SLC)
        m_s = ok[..., None] & (kpos <= tq[None, :, None, None, None])
        p_s = masked_softmax(s_s, m_s[:, :, :, None], (-2, -1))
        o_s = jnp.einsum('bthgkp,bhtkpd->bthgd', p_s.astype(vg.dtype), vg)
        st = q_pos0 + i0 - kw_pos0
        kk = lax.dynamic_slice_in_dim(kw_p, st, WINDOW + qb, axis=1)
        vv = lax.dynamic_slice_in_dim(vw_p, st, WINDOW + qb, axis=1)
        wpos = q_pos0 + i0 - WINDOW + jnp.arange(WINDOW + qb)
        m_w = ((wpos[None, :] >= kw_pos0) & (wpos[None, :] <= tq[:, None])
               & (wpos[None, :] > tq[:, None] - WINDOW))
        s_w = jnp.einsum('bthgd,bshd->bthgs', qq, kk) * scale
        p_w = masked_softmax(s_w, m_w[None, :, None, None, :], -1)
        o_w = jnp.einsum('bthgs,bshd->bthgd', p_w.astype(vv.dtype), vv)
        return o_s, o_w

    starts = jnp.arange(nq, dtype=jnp.int32) * qb
    o_s, o_w = lax.map(block_fn, (split_q(q), split_q(idx), split_q(sel_ok), starts))
    o_s = jnp.moveaxis(o_s, 0, 1).reshape(B, T, Hk, G, d)
    o_w = jnp.moveaxis(o_w, 0, 1).reshape(B, T, Hk, G, d)
    o = gates[..., 0:1] * o_c + gates[..., 1:2] * o_s + gates[..., 2:3] * o_w
    return o.reshape(B, T, Hk * G * d)


def hgrn_scan(q, k, v, logf, S0):
    B, T, H, dk = q.shape
    C = min(HG_CHUNK, T)
    nC = -(-T // C)
    pad = nC * C - T

    def prep(a):
        a = jnp.pad(a.astype(jnp.float32), ((0, 0), (0, pad), (0, 0), (0, 0)))
        return a.reshape(B, nC, C, H, a.shape[-1]).transpose(1, 0, 3, 2, 4)

    causal = jnp.tril(jnp.ones((C, C), dtype=bool))

    def step(S, inp):
        qc, kc, vc, gc = inp
        Gc = jnp.cumsum(gc, axis=2)
        diff = Gc[:, :, :, None, :] - Gc[:, :, None, :, :]
        dec = jnp.exp(jnp.where(causal[None, None, :, :, None], diff, -jnp.inf))
        A = jnp.einsum('bhtd,bhtsd->bhts', qc, dec * kc[:, :, None, :, :])
        o = jnp.einsum('bhts,bhsv->bhtv', A, vc) + jnp.einsum('bhtd,bhdv->bhtv', qc * jnp.exp(Gc), S)
        Gl = Gc[:, :, -1:, :]
        S = jnp.exp(Gl[:, :, 0, :, None]) * S + jnp.einsum('bhsd,bhsv->bhdv', kc * jnp.exp(Gl - Gc), vc)
        return S, o

    S, o = lax.scan(step, S0.astype(jnp.float32), (prep(q), prep(k), prep(v), prep(logf)))
    o = o.transpose(1, 0, 3, 2, 4).reshape(B, nC * C, H, -1)[:, :T]
    return o, S


def pool_mix(u, buf, pos0, pool_w, pool_scale):
    B, T, C = u.shape
    seq = u if buf is None else jnp.concatenate([buf.astype(u.dtype), u], axis=1)
    off = seq.shape[1] - T
    wmax = max(POOL_SIZES)
    cs = jnp.pad(jnp.cumsum(seq.astype(jnp.float32), axis=1), ((0, 0), (wmax, 0), (0, 0)))
    idx = off + jnp.arange(T) + wmax
    pos = pos0 + jnp.arange(T)
    parts = []
    for g, wsz in enumerate(POOL_SIZES):
        c0 = g * POOL_CH
        wsum = cs[:, idx, c0:c0 + POOL_CH] - cs[:, idx - wsz, c0:c0 + POOL_CH]
        cnt = jnp.minimum(wsz, pos + 1).astype(jnp.float32)
        parts.append(wsum / cnt[None, :, None] - u[:, :, c0:c0 + POOL_CH].astype(jnp.float32))
    dlt = jnp.stack(parts, axis=2)
    out = jnp.einsum('btgc,gcd->btgd', dlt, pool_w.astype(jnp.float32)).reshape(B, T, C)
    out = out * pool_scale.astype(jnp.float32)
    return out.astype(u.dtype), seq[:, -POOL_BUF:]


def layer_forward(x, pos0, mem, past, lb, norm_g, w_in, w_out, nsa_qn, nsa_kn, cmp_pe, cmp_w1, cmp_w2,
                  hg_on, pool_w, pool_scale, mem_norm, w_mem_kv, mem_qn, mem_kn):
    B, T, _ = x.shape
    dt = x.dtype
    h = rms_norm(x, norm_g)
    proj = h @ w_in
    parts, o = [], 0
    for wd in IN_WIDTHS:
        parts.append(proj[..., o:o + wd])
        o += wd
    q_n, kv_n, g_n, hq, hf, hi, u_pool, q_m, z = parts
    pos = pos0 + jnp.arange(T)

    q = rope(rms_norm(q_n.reshape(B, T, N_NSA_HEADS, HEAD_DIM), nsa_qn), pos)
    q = q.reshape(B, T, N_NSA_KV, NSA_G, HEAD_DIM)
    kv = kv_n.reshape(B, T, 6, N_NSA_KV, HEAD_DIM)
    keys = rms_norm(kv[:, :, 0::2], nsa_kn[:, None, :])
    keys = rope(keys.reshape(B, T, 3 * N_NSA_KV, HEAD_DIM), pos).reshape(B, T, 3, N_NSA_KV, HEAD_DIM)
    vals = kv[:, :, 1::2]
    nsa_rows = jnp.stack([keys[:, :, 0], vals[:, :, 0], keys[:, :, 1], vals[:, :, 1]], axis=2)
    win_rows = jnp.stack([keys[:, :, 2], vals[:, :, 2]], axis=2)
    if past is None:
        rows_all, win_all, kw_pos0, wb = nsa_rows, win_rows, pos0, min(WINDOW, T)
    else:
        rows_all = jnp.concatenate([past['nsa'].astype(dt), nsa_rows], axis=1)
        win_all = jnp.concatenate([past['win'].astype(dt), win_rows], axis=1)
        wb = past['win'].shape[1]
        kw_pos0 = pos0 - wb
    gates = jax.nn.sigmoid(g_n.astype(jnp.float32)).astype(dt).reshape(B, T, N_NSA_KV, NSA_G, 3)
    o_nsa = nsa_mix(q, rows_all[:, :, 0], rows_all[:, :, 1], rows_all[:, :, 2], rows_all[:, :, 3],
                    win_all[:, :, 0], win_all[:, :, 1], gates, pos0, kw_pos0, cmp_pe, cmp_w1, cmp_w2)
    new_win = win_all[:, -wb:]

    fr = hf.astype(jnp.float32).reshape(B, T, N_HG_HEADS, HG_DK)
    lbh = lb.reshape(N_HG_HEADS, HG_DK)
    logf = jnp.logaddexp(jnp.log(lbh), jnp.log1p(-lbh) + jax.nn.log_sigmoid(fr))
    k_in = (1.0 - lbh) * jax.nn.sigmoid(-fr)
    S0 = jnp.zeros((B, N_HG_HEADS, HG_DK, HG_DV), jnp.float32) if past is None else past['hgrn']
    o_hg, S_new = hgrn_scan(hq.reshape(B, T, N_HG_HEADS, HG_DK), k_in,
                            hi.reshape(B, T, N_HG_HEADS, HG_DV), logf, S0)
    o_hg = rms_norm(o_hg, hg_on.reshape(N_HG_HEADS, HG_DV)).reshape(B, T, GROUP_W).astype(dt)

    o_pool, new_pool = pool_mix(u_pool, None if past is None else past['pool'], pos0, pool_w, pool_scale)

    qm = rms_norm(q_m.reshape(B, T, N_MEM_HEADS, MEM_HD), mem_qn)
    if past is None:
        mkv = (rms_norm(mem, mem_norm) @ w_mem_kv).reshape(B, N_MEM, 2, N_MEM_HEADS, MEM_HD)
        mem_kv = jnp.stack([rms_norm(mkv[:, :, 0], mem_kn), mkv[:, :, 1]], axis=2)
    else:
        mem_kv = past['mem'].astype(dt)
    s_m = jnp.einsum('bthd,bmhd->bthm', qm, mem_kv[:, :, 0]).astype(jnp.float32) * MEM_HD ** -0.5
    p_m = jax.nn.softmax(s_m, axis=-1).astype(dt)
    o_mem = jnp.einsum('bthm,bmhd->bthd', p_m, mem_kv[:, :, 1]).reshape(B, T, GROUP_W)

    y = jnp.concatenate([o_nsa, o_hg, o_pool, o_mem], axis=-1) * jax.nn.silu(z)
    x = x + y @ w_out
    return x, nsa_rows, new_win, S_new.astype(dt), new_pool, mem_kv


def setup_inputs(seed: int = 0) -> dict:
    key = jax.random.key(seed)
    ks = jax.random.split(key, 26)

    def nrm(k, shape, s=1.0):
        return jax.random.normal(k, shape, jnp.float32) * s

    n_pages = PAST_LEN // PAGE_SIZE
    n_used = DEC_BATCH * n_pages
    n_phys = n_used + n_used // 4
    wb = min(WINDOW, PAST_LEN)
    page_table = jax.random.permutation(ks[8], n_phys)[:n_used].reshape(DEC_BATCH, n_pages).astype(jnp.int32)
    return {
        'x_prompt': nrm(ks[0], (BATCH, SEQ, D_MODEL)),
        'x_sample': nrm(ks[1], (DEC_BATCH, DEC_SEQ, D_MODEL)),
        'mem_prompt': nrm(ks[2], (BATCH, N_MEM, D_MODEL)),
        'cache_nsa': nrm(ks[3], (DEPTH, n_phys, PAGE_SIZE, 4, N_NSA_KV, HEAD_DIM)),
        'cache_nsa_win': nrm(ks[4], (DEPTH, DEC_BATCH, wb, 2, N_NSA_KV, HEAD_DIM)),
        'state_hgrn': nrm(ks[5], (DEPTH, DEC_BATCH, N_HG_HEADS, HG_DK, HG_DV), 0.3),
        'state_pool': nrm(ks[6], (DEPTH, DEC_BATCH, POOL_BUF, GROUP_W)),
        'cache_mem': nrm(ks[7], (DEPTH, DEC_BATCH, N_MEM, 2, N_MEM_HEADS, MEM_HD)),
        'page_table': page_table,
        'norm_g': 1.0 + nrm(ks[9], (DEPTH, D_MODEL), 0.02),
        'w_in': nrm(ks[10], (DEPTH, D_MODEL, IN_W), D_MODEL ** -0.5),
        'w_out': nrm(ks[11], (DEPTH, MIX_W, D_MODEL), MIX_W ** -0.5),
        'nsa_qn': 1.0 + nrm(ks[12], (DEPTH, HEAD_DIM), 0.02),
        'nsa_kn': 1.0 + nrm(ks[13], (DEPTH, 3, HEAD_DIM), 0.02),
        'cmp_pe': nrm(ks[14], (DEPTH, 2, L_CMP, HEAD_DIM), 0.1),
        'cmp_w1': nrm(ks[15], (DEPTH, 2, L_CMP * HEAD_DIM, CMP_HID), (L_CMP * HEAD_DIM) ** -0.5),
        'cmp_w2': nrm(ks[16], (DEPTH, 2, CMP_HID, HEAD_DIM), CMP_HID ** -0.5),
        'hg_lb': nrm(ks[17], (DEPTH, GROUP_W), 1.0),
        'hg_on': 1.0 + nrm(ks[18], (DEPTH, GROUP_W), 0.02),
        'pool_w': nrm(ks[19], (DEPTH, len(POOL_SIZES), POOL_CH, POOL_CH), POOL_CH ** -0.5),
        'pool_scale': 1.0 + nrm(ks[20], (DEPTH, GROUP_W), 0.02),
        'mem_norm': 1.0 + nrm(ks[21], (DEPTH, D_MODEL), 0.02),
        'w_mem_kv': nrm(ks[22], (DEPTH, D_MODEL, 2 * GROUP_W), D_MODEL ** -0.5),
        'mem_qn': 1.0 + nrm(ks[23], (DEPTH, MEM_HD), 0.02),
        'mem_kn': 1.0 + nrm(ks[24], (DEPTH, MEM_HD), 0.02),
    }


def reference(x_prompt, x_sample, mem_prompt, cache_nsa, cache_nsa_win, state_hgrn, state_pool, cache_mem,
              page_table, norm_g, w_in, w_out, nsa_qn, nsa_kn, cmp_pe, cmp_w1, cmp_w2, hg_lb, hg_on,
              pool_w, pool_scale, mem_norm, w_mem_kv, mem_qn, mem_kn):
    lbs = jnp.cumsum(jax.nn.softmax(hg_lb.astype(jnp.float32), axis=0), axis=0)
    lbs = lbs - lbs[0:1]
    dec_b, n_pages = page_table.shape
    xp, xs = x_prompt, x_sample
    p_nsa, p_win, p_hg, p_pool, p_mem = [], [], [], [], []
    s_nsa, s_win, s_hg, s_pool = [], [], [], []
    for l in range(DEPTH):
        lw = (norm_g[l], w_in[l], w_out[l], nsa_qn[l], nsa_kn[l], cmp_pe[l], cmp_w1[l], cmp_w2[l],
              hg_on[l], pool_w[l], pool_scale[l], mem_norm[l], w_mem_kv[l], mem_qn[l], mem_kn[l])
        xp, a, b, c, d, e = layer_forward(xp, 0, mem_prompt, None, lbs[l], *lw)
        p_nsa.append(a); p_win.append(b); p_hg.append(c); p_pool.append(d); p_mem.append(e)
        past = {
            'nsa': cache_nsa[l][page_table].reshape(dec_b, n_pages * PAGE_SIZE, 4, N_NSA_KV, HEAD_DIM),
            'win': cache_nsa_win[l],
            'hgrn': state_hgrn[l],
            'pool': state_pool[l],
            'mem': cache_mem[l],
        }
        xs, a, b, c, d, _ = layer_forward(xs, PAST_LEN, None, past, lbs[l], *lw)
        s_nsa.append(a); s_win.append(b); s_hg.append(c); s_pool.append(d)
    return (xp, xs,
            jnp.stack(p_nsa), jnp.stack(p_win), jnp.stack(p_hg), jnp.stack(p_pool), jnp.stack(p_mem),
            jnp.stack(s_nsa), jnp.stack(s_win), jnp.stack(s_hg), jnp.stack(s_pool))
```

```python
import functools

import numpy as np
import jax
import jax.numpy as jnp
from jax import lax
from jax.experimental import pallas as pl
from jax.experimental.pallas import tpu as pltpu

D_MODEL = 1024
HEAD_DIM = 64
GROUP_W = 256
N_HEADS = 4
L_CMP = 32
L_SLC = 64
N_SEL = 16
WINDOW = 512
CMP_HID = 256
POOL_SIZES = (2, 4, 8, 16)
POOL_BUF = 15
N_MEM = 256
PAGE_SIZE = 128
ROPE_THETA = 10000.0
EPS = 1e-6
QK_SCALE = HEAD_DIM ** -0.5

PROJ_W = 3072
HG_CHUNK = 256
NSA_TQ = 256
NEG = -1e30
VMEM_LIMIT = 56 * 1024 * 1024

F32 = jnp.float32
BF16 = jnp.bfloat16
HI = lax.Precision.HIGHEST


def _cparams(sem):
    return pltpu.CompilerParams(dimension_semantics=sem, vmem_limit_bytes=VMEM_LIMIT)


def _bdot(a, b):
    return jnp.dot(a.astype(BF16), b.astype(BF16), preferred_element_type=F32)


def _bdot_nt(a, b):
    return lax.dot_general(a.astype(BF16), b.astype(BF16), (((1,), (1,)), ((), ())), preferred_element_type=F32)


def _hdot(a, b):
    return jnp.dot(a, b, precision=HI, preferred_element_type=F32)


def _hdot_nt(a, b):
    return lax.dot_general(a, b, (((1,), (1,)), ((), ())), precision=HI, preferred_element_type=F32)


def _dot2(a, b16):
    hi = a.astype(BF16)
    lo = (a - hi.astype(F32)).astype(BF16)
    return jnp.dot(hi, b16, preferred_element_type=F32) + jnp.dot(lo, b16, preferred_element_type=F32)


def _dot2_left(a16, b):
    hi = b.astype(BF16)
    lo = (b - hi.astype(F32)).astype(BF16)
    return jnp.dot(a16, hi, preferred_element_type=F32) + jnp.dot(a16, lo, preferred_element_type=F32)


def _seg_rms(x, bd16, gain):
    ms = _dot2(x * x, bd16) * (1.0 / HEAD_DIM)
    return x * lax.rsqrt(ms + EPS) * gain


def _rope(x, cos, sin_signed):
    w = x.shape[1]
    lane = lax.broadcasted_iota(jnp.int32, x.shape, 1)
    first = (lane & 32) == 0
    rot = jnp.where(first, pltpu.roll(x, w - 32, 1), pltpu.roll(x, 32, 1))
    return x * cos + rot * sin_signed


def _sigmoid(x):
    return jax.nn.sigmoid(x)


def _lane_head(shape):
    return lax.broadcasted_iota(jnp.int32, shape, 1) // HEAD_DIM


def _proj_kernel(x_ref, g_ref, w_ref, cq_ref, sq_ref, ckv_ref, skv_ref, qn_ref, kn_ref, mqn_ref, bd256_ref, bd384_ref,
                 q_out, nsa_out, win_out, gate_out, hq_out, hf_out, hi_out, u_out, qm_out, z_out, *, precise):
    x = x_ref[...]
    ms = jnp.mean(x * x, axis=-1, keepdims=True)
    h = x * lax.rsqrt(ms + EPS) * g_ref[...]
    proj = _hdot(h, w_ref[...]) if precise else _bdot(h, w_ref[...])
    bd256 = bd256_ref[...]
    q = _rope(_seg_rms(proj[:, 0:256], bd256, qn_ref[...]), cq_ref[...], sq_ref[...])
    q_out[...] = q
    kv = proj[:, 256:640]
    kvn = _rope(_seg_rms(kv, bd384_ref[...], kn_ref[...]), ckv_ref[...], skv_ref[...])
    lane = lax.broadcasted_iota(jnp.int32, kv.shape, 1)
    is_key = ((lane // HEAD_DIM) & 1) == 0
    kv = jnp.where(is_key, kvn, kv)
    nsa_out[...] = kv[:, 0:256]
    win_out[...] = kv[:, 256:384]
    gate_out[...] = _sigmoid(proj[:, 640:768])
    hq_out[...] = proj[:, 768:1024]
    hf_out[...] = proj[:, 1024:1280]
    hi_out[...] = proj[:, 1280:1536]
    u_out[...] = proj[:, 1536:1792]
    qm_out[...] = _seg_rms(proj[:, 1792:2048], bd256, mqn_ref[...])
    z_out[...] = proj[:, 2048:3072]


def _proj_call(x, g, w, cq, sq, ckv, skv, qn, kn, mqn, bd256, bd384, *, tm, t_tiles, precise):
    n = x.shape[0]
    row = lambda wd: pl.BlockSpec((tm, wd), lambda i: (i, 0))
    tab = lambda wd: pl.BlockSpec((tm, wd), lambda i: (i % t_tiles, 0))
    full = lambda a: pl.BlockSpec(a.shape, lambda i: (0, 0))
    widths = (256, 256, 128, 128, 256, 256, 256, 256, 256, 1024)
    return pl.pallas_call(
        functools.partial(_proj_kernel, precise=precise),
        grid=(n // tm,),
        in_specs=[row(D_MODEL), full(g), full(w), tab(256), tab(256), tab(384), tab(384),
                  full(qn), full(kn), full(mqn), full(bd256), full(bd384)],
        out_specs=[row(wd) for wd in widths],
        out_shape=[jax.ShapeDtypeStruct((n, wd), F32) for wd in widths],
        compiler_params=_cparams(("parallel",)),
    )(x, g, w, cq, sq, ckv, skv, qn, kn, mqn, bd256, bd384)


def _cmp_kernel(x_ref, pe_ref, w1_ref, w2_ref, o_ref):
    x = x_ref[...] + pe_ref[...]
    h = _bdot(x, w1_ref[...])
    h = h * _sigmoid(h)
    o_ref[...] = _bdot(h, w2_ref[...])


def _cmp_call(x, pe, w1, w2, *, tr, n_rows, row_off):
    off = row_off // tr
    full = lambda a: pl.BlockSpec(a.shape, lambda i: (0, 0))
    return pl.pallas_call(
        _cmp_kernel,
        grid=(n_rows // tr,),
        in_specs=[pl.BlockSpec((tr, x.shape[1]), lambda i: (i + off, 0)), full(pe), full(w1), full(w2)],
        out_specs=pl.BlockSpec((tr, 128), lambda i: (i, 0)),
        out_shape=jax.ShapeDtypeStruct((n_rows, 128), F32),
        compiler_params=_cparams(("parallel",)),
    )(x, pe, w1, w2)


def _topk_mask(score, k, idx_f):
    sel = jnp.zeros(score.shape, jnp.bool_)
    s = score
    big = jnp.float32(1e9)
    picks = []
    for _ in range(k):
        m = jnp.max(s, axis=1, keepdims=True)
        cand = jnp.where(s == m, idx_f, big)
        pick = jnp.min(cand, axis=1, keepdims=True)
        hit = idx_f == pick
        sel = sel | hit
        s = jnp.where(hit, -jnp.inf, s)
        picks.append(pick)
    return sel, picks


def _softmax_update(s, mask, m, l, acc, v):
    s = jnp.where(mask, s, NEG)
    m_new = jnp.maximum(m, jnp.max(s, axis=1, keepdims=True))
    alpha = jnp.exp(m - m_new)
    p = jnp.where(mask, jnp.exp(s - m_new), 0.0)
    l = alpha * l + jnp.sum(p, axis=1, keepdims=True)
    acc = alpha * acc + _bdot(p, v)
    return m_new, l, acc


def _nsa_kernel(q_ref, nsa_ref, win_ref, gate_ref, cc_ref, o_ref, qs_ref, *, tq):
    i = pl.program_id(1)
    t0 = i * tq
    r4 = N_HEADS * tq
    q = q_ref[0] * QK_SCALE
    for h in range(N_HEADS):
        qs_ref[h * tq:(h + 1) * tq, :] = q[:, h * HEAD_DIM:(h + 1) * HEAD_DIM].astype(BF16)
    qs = qs_ref[...]
    trow = t0 + lax.broadcasted_iota(jnp.int32, (r4, 1), 0) % tq

    cc = cc_ref[0]
    s_c = _bdot_nt(qs, cc[:, 0:HEAD_DIM])
    n_idx = lax.broadcasted_iota(jnp.int32, s_c.shape, 1)
    done = (n_idx + 1) * L_CMP - 1 <= trow
    s_m = jnp.where(done, s_c, NEG)
    m_c = jnp.max(s_m, axis=1, keepdims=True)
    m_c = jnp.where(m_c < 0.5 * NEG, 0.0, m_c)
    p_c = jnp.where(done, jnp.exp(s_m - m_c), 0.0)
    p_c = p_c / jnp.maximum(jnp.sum(p_c, axis=1, keepdims=True), 1e-30)
    o_c = _bdot(p_c, cc[:, HEAD_DIM:2 * HEAD_DIM])

    imp = p_c[0:tq] + p_c[tq:2 * tq] + p_c[2 * tq:3 * tq] + p_c[3 * tq:4 * tq]
    imp2 = imp + pltpu.roll(imp, 127, 1)
    lane = lax.broadcasted_iota(jnp.int32, imp.shape, 1)
    tq_pos = t0 + lax.broadcasted_iota(jnp.int32, (tq, 1), 0)
    blk_t = tq_pos // L_SLC
    j = lane // 2
    n_slc = nsa_ref.shape[1] // L_SLC
    valid = ((lane & 1) == 0) & (j < n_slc)
    avail = valid & (j <= blk_t)
    forced = (j == 0) | (j == blk_t) | (j == blk_t - 1)
    score = jnp.where(avail, jnp.where(forced, jnp.inf, imp2), -jnp.inf)
    score = jnp.where(valid, score, -jnp.inf)
    sel, _ = _topk_mask(score, N_SEL, lane.astype(F32))
    sel16 = jnp.where(sel & avail, 1.0, 0.0).astype(BF16)

    zero_m = jnp.full((r4, 1), NEG, F32)
    zero_l = jnp.zeros((r4, 1), F32)
    zero_a = jnp.zeros((r4, HEAD_DIM), F32)

    def kpos_of(c):
        return c * tq + lax.broadcasted_iota(jnp.int32, (r4, tq), 1)

    def sel_body(c, carry):
        m, l, acc = carry
        start = pl.multiple_of(c * tq, tq)
        kv = nsa_ref[0, pl.ds(start, tq), :]
        s = _bdot_nt(qs, kv[:, 128:192])
        el = lax.broadcasted_iota(jnp.int32, (128, tq), 0)
        ek = c * tq + lax.broadcasted_iota(jnp.int32, (128, tq), 1)
        expand = jnp.where(el == 2 * (ek // L_SLC), 1.0, 0.0).astype(BF16)
        msel = jnp.dot(sel16, expand, preferred_element_type=F32)
        msel = jnp.concatenate([msel] * N_HEADS, axis=0)
        mask = (msel > 0.5) & (kpos_of(c) <= trow)
        return _softmax_update(s, mask, m, l, acc, kv[:, 192:256])

    m_s, l_s, a_s = lax.fori_loop(0, i + 1, sel_body, (zero_m, zero_l, zero_a))
    o_s = a_s / jnp.maximum(l_s, 1e-30)

    def win_body(c, carry):
        m, l, acc = carry
        start = pl.multiple_of(c * tq, tq)
        kv = win_ref[0, pl.ds(start, tq), :]
        s = _bdot_nt(qs, kv[:, 0:HEAD_DIM])
        kpos = kpos_of(c)
        mask = (kpos <= trow) & (kpos > trow - WINDOW)
        return _softmax_update(s, mask, m, l, acc, kv[:, HEAD_DIM:2 * HEAD_DIM])

    c_lo = jnp.maximum(i - WINDOW // tq, 0)
    m_w, l_w, a_w = lax.fori_loop(c_lo, i + 1, win_body, (zero_m, zero_l, zero_a))
    o_w = a_w / jnp.maximum(l_w, 1e-30)

    g = gate_ref[0]
    outs = []
    for h in range(N_HEADS):
        rows = slice(h * tq, (h + 1) * tq)
        outs.append(g[:, 3 * h:3 * h + 1] * o_c[rows] + g[:, 3 * h + 1:3 * h + 2] * o_s[rows]
                    + g[:, 3 * h + 2:3 * h + 3] * o_w[rows])
    o_ref[0] = jnp.concatenate(outs, axis=1)


def _nsa_call(q, nsa, win, gates, cc, *, tq):
    b, t, _ = q.shape
    return pl.pallas_call(
        functools.partial(_nsa_kernel, tq=tq),
        grid=(b, t // tq),
        in_specs=[pl.BlockSpec((1, tq, 256), lambda bi, i: (bi, i, 0)),
                  pl.BlockSpec((1, t, 256), lambda bi, i: (bi, 0, 0)),
                  pl.BlockSpec((1, t, 128), lambda bi, i: (bi, 0, 0)),
                  pl.BlockSpec((1, tq, 128), lambda bi, i: (bi, i, 0)),
                  pl.BlockSpec((1, 128, 128), lambda bi, i: (bi, 0, 0))],
        out_specs=pl.BlockSpec((1, tq, 256), lambda bi, i: (bi, i, 0)),
        out_shape=jax.ShapeDtypeStruct((b, t, 256), F32),
        scratch_shapes=[pltpu.VMEM((N_HEADS * tq, HEAD_DIM), BF16)],
        compiler_params=_cparams(("parallel", "arbitrary")),
    )(q, nsa, win, gates, cc)


def _hgrn_consts(c=HG_CHUNK):
    t = np.arange(c)[:, None]
    k = np.arange(c)[None, :]
    blocks = [k <= t, k > t]
    masks = []
    m = c // 2
    while m >= 1:
        blk = t // (2 * m)
        upper = (t % (2 * m)) >= m
        ref = blk * 2 * m + m - 1
        blocks.append(upper & (k > ref) & (k <= t))
        blocks.append((~upper) & (k > t) & (k <= ref))
        masks.append(upper & ((k // (2 * m)) == blk) & ((k % (2 * m)) < m))
        m //= 2
    blocks.append(np.zeros((c, c), bool))
    blocks.append(np.zeros((c, c), bool))
    masks.append(t == k)
    d = np.concatenate(blocks, 0).astype(np.float32)
    mk = np.stack(masks).astype(np.float32)
    return d, mk


def _log_decay(fr, la, lc):
    ls = jnp.minimum(fr, 0.0) - jnp.log1p(jnp.exp(-jnp.abs(fr)))
    b = lc + ls
    return jnp.maximum(la, b) + jnp.log1p(jnp.exp(-jnp.abs(la - b)))


def _hgrn_kernel(hq_ref, hf_ref, hi_ref, la_ref, lc_ref, oml_ref, on_ref, d_ref, mk_ref, bd_ref,
                 o_ref, st_ref, st_scr):
    c = HG_CHUNK
    jt = pl.program_id(1)

    @pl.when(jt == 0)
    def _():
        st_scr[...] = jnp.zeros_like(st_scr)

    q = hq_ref[0]
    fr = hf_ref[0]
    v = hi_ref[0]
    logf = _log_decay(fr, la_ref[...], lc_ref[...])
    k = oml_ref[...] * _sigmoid(-fr)
    e_all = _dot2_left(d_ref[...], logf)
    gc = e_all[0:c]
    esuf = e_all[c:2 * c]
    lane_h = _lane_head((c, GROUP_W))
    n_lvl = mk_ref.shape[0]
    a_tot = [jnp.zeros((c, c), F32) for _ in range(N_HEADS)]
    for lv in range(n_lvl):
        eq = e_all[(2 + 2 * lv) * c:(3 + 2 * lv) * c]
        ek = e_all[(3 + 2 * lv) * c:(4 + 2 * lv) * c]
        qt = q * jnp.exp(eq)
        kt = (k * jnp.exp(ek)).astype(BF16)
        qs = jnp.concatenate([jnp.where(lane_h == h, qt, 0.0).astype(BF16) for h in range(N_HEADS)], axis=0)
        a_l = lax.dot_general(qs, kt, (((1,), (1,)), ((), ())), preferred_element_type=F32)
        mk = mk_ref[lv]
        for h in range(N_HEADS):
            a_tot[h] = a_tot[h] + mk * a_l[h * c:(h + 1) * c]
    a_all = jnp.concatenate(a_tot, axis=0)
    r = _bdot(a_all, v)
    o = jnp.zeros((c, GROUP_W), F32)
    for h in range(N_HEADS):
        o = o + jnp.where(lane_h == h, r[h * c:(h + 1) * c], 0.0)
    st = st_scr[...]
    o = o + _bdot_nt(q * jnp.exp(gc), st)
    gl = gc[c - 1:c, :]
    khat = k * jnp.exp(esuf)
    upd = jnp.dot(v.T.astype(BF16), khat.astype(BF16), preferred_element_type=F32)
    row_h = lax.broadcasted_iota(jnp.int32, (GROUP_W, GROUP_W), 0) // HEAD_DIM
    col_h = _lane_head((GROUP_W, GROUP_W))
    st_new = jnp.exp(gl) * st + jnp.where(row_h == col_h, upd, 0.0)
    st_scr[...] = st_new
    st_ref[0] = st_new
    o_ref[0] = _seg_rms(o, bd_ref[...], on_ref[...])


def _hgrn_call(hq, hf, hi, la, lc, oml, on, d16, mk, bd256):
    b, t, _ = hq.shape
    c = HG_CHUNK
    row = pl.BlockSpec((1, c, GROUP_W), lambda bi, j: (bi, j, 0))
    full2 = lambda a: pl.BlockSpec(a.shape, lambda bi, j: (0, 0))
    return pl.pallas_call(
        _hgrn_kernel,
        grid=(b, t // c),
        in_specs=[row, row, row, full2(la), full2(lc), full2(oml), full2(on), full2(d16),
                  pl.BlockSpec(mk.shape, lambda bi, j: (0, 0, 0)), full2(bd256)],
        out_specs=[row, pl.BlockSpec((1, GROUP_W, GROUP_W), lambda bi, j: (bi, 0, 0))],
        out_shape=[jax.ShapeDtypeStruct((b, t, GROUP_W), F32), jax.ShapeDtypeStruct((b, GROUP_W, GROUP_W), F32)],
        scratch_shapes=[pltpu.VMEM((GROUP_W, GROUP_W), F32)],
        compiler_params=_cparams(("parallel", "arbitrary")),
    )(hq, hf, hi, la, lc, oml, on, d16, mk, bd256)


def _memkv_kernel(m_ref, g_ref, w_ref, kn_ref, bd_ref, o_ref):
    x = m_ref[0]
    ms = jnp.mean(x * x, axis=-1, keepdims=True)
    h = x * lax.rsqrt(ms + EPS) * g_ref[...]
    kv = _bdot(h, w_ref[...])
    o_ref[0, :, 0:GROUP_W] = _seg_rms(kv[:, 0:GROUP_W], bd_ref[...], kn_ref[...])
    o_ref[0, :, GROUP_W:2 * GROUP_W] = kv[:, GROUP_W:2 * GROUP_W]


def _memkv_call(mem, g, w, kn, bd256):
    b = mem.shape[0]
    full = lambda a: pl.BlockSpec(a.shape, lambda i: (0, 0))
    return pl.pallas_call(
        _memkv_kernel,
        grid=(b,),
        in_specs=[pl.BlockSpec((1, N_MEM, D_MODEL), lambda i: (i, 0, 0)), full(g), full(w), full(kn), full(bd256)],
        out_specs=pl.BlockSpec((1, N_MEM, 2 * GROUP_W), lambda i: (i, 0, 0)),
        out_shape=jax.ShapeDtypeStruct((b, N_MEM, 2 * GROUP_W), F32),
        compiler_params=_cparams(("parallel",)),
    )(mem, g, w, kn, bd256)


def _final_kernel(x_ref, onsa_ref, ohg_ref, u_ref, halo_ref, qm_ref, z_ref, mkv_ref, pw_ref, ps_ref, wo_ref,
                  y_ref, ext_scr, *, tm):
    jt = pl.program_id(1)
    u = u_ref[0]
    halo = halo_ref[0]
    ext_scr[0:16, :] = jnp.where(jt == 0, 0.0, halo)
    ext_scr[16:16 + tm, :] = u
    pos = jt * tm + lax.broadcasted_iota(jnp.int32, (tm, 1), 0)
    lane_g = _lane_head((tm, GROUP_W))
    run = u
    pooled = jnp.zeros((tm, GROUP_W), F32)
    nxt = 1
    for gi, wsz in enumerate(POOL_SIZES):
        for sh in range(nxt, wsz):
            run = run + ext_scr[16 - sh:16 - sh + tm, :]
        nxt = wsz
        cnt = jnp.minimum(wsz, pos + 1).astype(F32)
        pooled = jnp.where(lane_g == gi, run / cnt, pooled)
    o_pool = _bdot(pooled - u, pw_ref[...]) * ps_ref[...]

    qm = qm_ref[0] * QK_SCALE
    mkv = mkv_ref[0]
    heads = []
    for h in range(N_HEADS):
        sl = slice(h * HEAD_DIM, (h + 1) * HEAD_DIM)
        s = _bdot_nt(qm[:, sl], mkv[:, sl])
        s = s - jnp.max(s, axis=1, keepdims=True)
        p = jnp.exp(s)
        p = p / jnp.sum(p, axis=1, keepdims=True)
        heads.append(_bdot(p, mkv[:, GROUP_W + h * HEAD_DIM:GROUP_W + (h + 1) * HEAD_DIM]))
    o_mem = jnp.concatenate(heads, axis=1)

    z = z_ref[0]
    y = jnp.concatenate([onsa_ref[0], ohg_ref[0], o_pool, o_mem], axis=1) * (z * _sigmoid(z))
    y_ref[0] = x_ref[0] + _bdot(y, wo_ref[...])


def _final_call(x, onsa, ohg, u, qm, z, mkv, pw, ps, wo, *, tm):
    b, t, _ = x.shape
    row = lambda wd: pl.BlockSpec((1, tm, wd), lambda bi, j: (bi, j, 0))
    full2 = lambda a: pl.BlockSpec(a.shape, lambda bi, j: (0, 0))
    hb = tm // 16
    return pl.pallas_call(
        functools.partial(_final_kernel, tm=tm),
        grid=(b, t // tm),
        in_specs=[row(D_MODEL), row(GROUP_W), row(GROUP_W), row(GROUP_W),
                  pl.BlockSpec((1, 16, GROUP_W), lambda bi, j: (bi, jnp.maximum(j * hb - 1, 0), 0)),
                  row(GROUP_W), row(D_MODEL),
                  pl.BlockSpec((1, N_MEM, 2 * GROUP_W), lambda bi, j: (bi, 0, 0)),
                  full2(pw), full2(ps), full2(wo)],
        out_specs=row(D_MODEL),
        out_shape=jax.ShapeDtypeStruct((b, t, D_MODEL), F32),
        scratch_shapes=[pltpu.VMEM((tm + 16, GROUP_W), F32)],
        compiler_params=_cparams(("parallel", "arbitrary")),
    )(x, onsa, ohg, u, u, qm, z, mkv, pw, ps, wo)


def _head_rows(row):
    r8 = lax.broadcasted_iota(jnp.int32, (8, GROUP_W), 0)
    qm = jnp.where(r8 == _lane_head((8, GROUP_W)), jnp.broadcast_to(row, (8, GROUP_W)), 0.0)
    return qm[:, 0:64] + qm[:, 64:128] + qm[:, 128:192] + qm[:, 192:256]


def _scmp_kernel(pt_ref, q_ref, cc_hbm, idx_ref, oc_ref, g_scr, a_scr, b_scr, sem, *, nb, npg):
    def copy(n):
        b = n // npg
        pg = pt_ref[b, n - b * npg]
        return pltpu.make_async_copy(cc_hbm.at[pl.ds(pg, 1)], g_scr.at[pl.ds(n, 1)], sem)

    def start(n, c):
        copy(n).start()
        return c

    def wait(n, c):
        copy(n).wait()
        return c

    lax.fori_loop(0, nb * npg, start, 0)
    lax.fori_loop(0, nb * npg, wait, 0)

    def per_batch(b, c):
        gb = g_scr[pl.ds(pl.multiple_of(b * npg, npg), npg), :]
        qh = _head_rows(q_ref[pl.ds(b, 1), :]) * QK_SCALE
        ss = [_hdot_nt(qh, gb[:, r * 128:r * 128 + HEAD_DIM]) for r in range(4)]
        m = ss[0].max(axis=1, keepdims=True)
        for r in range(1, 4):
            m = jnp.maximum(m, ss[r].max(axis=1, keepdims=True))
        ps = [jnp.exp(s - m) for s in ss]
        l = ps[0].sum(axis=1, keepdims=True)
        for r in range(1, 4):
            l = l + ps[r].sum(axis=1, keepdims=True)
        inv = 1.0 / jnp.maximum(l, 1e-30)
        ps = [p * inv for p in ps]
        oc = _hdot(ps[0], gb[:, HEAD_DIM:128])
        for r in range(1, 4):
            oc = oc + _hdot(ps[r], gb[:, r * 128 + HEAD_DIM:(r + 1) * 128])
        oc_ref[pl.ds(pl.multiple_of(b * 8, 8), 8), :] = oc
        hmask = lax.broadcasted_iota(jnp.int32, ss[0].shape, 0) < N_HEADS
        imps = [jnp.sum(jnp.where(hmask, p, 0.0), axis=0, keepdims=True) for p in ps]
        a_scr[pl.ds(b, 1), :] = imps[0] + imps[1]
        b_scr[pl.ds(b, 1), :] = imps[2] + imps[3]
        return c

    lax.fori_loop(0, nb, per_batch, 0)

    score = jnp.concatenate([a_scr[...], b_scr[...]], axis=1)
    lane = lax.broadcasted_iota(jnp.int32, score.shape, 1)
    jidx = jnp.where(lane < npg, 2 * lane, 2 * (lane - npg) + 1)
    forced = (jidx == 0) | (jidx == 2 * npg - 1)
    score = jnp.where(forced, jnp.inf, score)
    _, picks = _topk_mask(score, N_SEL - 1, jidx.astype(F32))
    out_lane = lax.broadcasted_iota(jnp.int32, (nb, 128), 1)
    out = jnp.zeros((nb, 128), jnp.int32)
    for r, pk in enumerate(picks):
        out = jnp.where(out_lane == r, pk.astype(jnp.int32), out)
    idx_ref[...] = out


def _scmp_call(page_table, q, cc_rows):
    nb, npg = page_table.shape
    return pl.pallas_call(
        functools.partial(_scmp_kernel, nb=nb, npg=npg),
        grid_spec=pltpu.PrefetchScalarGridSpec(
            num_scalar_prefetch=1,
            grid=(1,),
            in_specs=[pl.BlockSpec(q.shape, lambda i, pt: (0, 0)), pl.BlockSpec(memory_space=pl.ANY)],
            out_specs=[pl.BlockSpec((nb, 128), lambda i, pt: (0, 0)), pl.BlockSpec((nb * 8, HEAD_DIM), lambda i, pt: (0, 0))],
            scratch_shapes=[pltpu.VMEM((nb * npg, 512), F32), pltpu.VMEM((nb, 128), F32), pltpu.VMEM((nb, 128), F32),
                            pltpu.SemaphoreType.DMA(())],
        ),
        out_shape=[jax.ShapeDtypeStruct((nb, 128), jnp.int32), jax.ShapeDtypeStruct((nb * 8, HEAD_DIM), F32)],
        compiler_params=_cparams(("arbitrary",)),
    )(page_table, q, cc_rows)


def _ssel_kernel(pt_ref, idx_ref, blk_ref, q_ref, new_ref, wc_ref, wnew_ref, gate_ref, oc_ref, o_ref,
                 m_scr, l_scr, a_scr, *, n_gather):
    r = pl.program_id(1)

    @pl.when(r == 0)
    def _():
        m_scr[...] = jnp.full_like(m_scr, NEG)
        l_scr[...] = jnp.zeros_like(l_scr)
        a_scr[...] = jnp.zeros_like(a_scr)

    qh = _head_rows(q_ref[0]) * QK_SCALE
    blk = blk_ref[0]
    s = _hdot_nt(qh, blk[:, 128:192])
    m_old = m_scr[...]
    m_new = jnp.maximum(m_old, s.max(axis=1, keepdims=True))
    alpha = jnp.exp(m_old - m_new)
    p = jnp.exp(s - m_new)
    l_scr[...] = alpha * l_scr[...] + p.sum(axis=1, keepdims=True)
    a_scr[...] = alpha * a_scr[...] + _hdot(p, blk[:, 192:256])
    m_scr[...] = m_new

    @pl.when(r == n_gather - 1)
    def _():
        new = new_ref[0]
        s_n = jnp.sum(qh * new[:, 128:192], axis=1, keepdims=True)
        m0 = m_scr[...]
        m1 = jnp.maximum(m0, s_n)
        al = jnp.exp(m0 - m1)
        p_n = jnp.exp(s_n - m1)
        l1 = al * l_scr[...] + p_n
        o_s = (al * a_scr[...] + p_n * new[:, 192:256]) / jnp.maximum(l1, 1e-30)
        wc = wc_ref[0]
        wnew = wnew_ref[0]
        s_w = _hdot_nt(qh, wc[:, 0:HEAD_DIM])
        keep = lax.broadcasted_iota(jnp.int32, s_w.shape, 1) >= wc.shape[0] + 1 - WINDOW
        s_w = jnp.where(keep, s_w, NEG)
        s_wn = jnp.sum(qh * wnew[:, 0:HEAD_DIM], axis=1, keepdims=True)
        m_w = jnp.maximum(s_w.max(axis=1, keepdims=True), s_wn)
        p_w = jnp.where(keep, jnp.exp(s_w - m_w), 0.0)
        p_wn = jnp.exp(s_wn - m_w)
        l_w = p_w.sum(axis=1, keepdims=True) + p_wn
        o_w = (_hdot(p_w, wc[:, HEAD_DIM:128]) + p_wn * wnew[:, HEAD_DIM:128]) / jnp.maximum(l_w, 1e-30)
        o_c = oc_ref[0]
        g = gate_ref[0]
        outs = []
        for h in range(N_HEADS):
            outs.append(g[:, 3 * h:3 * h + 1] * o_c[h:h + 1] + g[:, 3 * h + 1:3 * h + 2] * o_s[h:h + 1]
                        + g[:, 3 * h + 2:3 * h + 3] * o_w[h:h + 1])
        o_ref[0] = jnp.concatenate(outs, axis=1)


def _ssel_call(page_table, idx, cache_blocks, q, new_rows, win_cache, win_new, gates, o_c, *, layer, n_phys):
    nb = q.shape[0]
    n_gather = N_SEL - 1
    wb = win_cache.shape[1]
    blk_off = layer * n_phys * 2

    def blk_map(b, r, pt, ix):
        j = ix[b, r]
        return (blk_off + pt[b, j // 2] * 2 + j % 2, 0, 0)

    per_b = lambda rows, wd: pl.BlockSpec((1, rows, wd), lambda b, r, pt, ix: (b, 0, 0))
    return pl.pallas_call(
        functools.partial(_ssel_kernel, n_gather=n_gather),
        grid_spec=pltpu.PrefetchScalarGridSpec(
            num_scalar_prefetch=2,
            grid=(nb, n_gather),
            in_specs=[pl.BlockSpec((1, L_SLC, 256), blk_map), per_b(1, 256), per_b(1, 256),
                      pl.BlockSpec((1, wb, 128), lambda b, r, pt, ix: (layer * nb + b, 0, 0)),
                      per_b(1, 128), per_b(1, 128), per_b(8, HEAD_DIM)],
            out_specs=per_b(1, 256),
            scratch_shapes=[pltpu.VMEM((8, 1), F32), pltpu.VMEM((8, 1), F32), pltpu.VMEM((8, HEAD_DIM), F32)],
        ),
        out_shape=jax.ShapeDtypeStruct((nb, 1, 256), F32),
        compiler_params=_cparams(("arbitrary", "arbitrary")),
    )(page_table, idx, cache_blocks, q, new_rows, win_cache, win_new, gates, o_c)


def _to_col(row, eye):
    return jnp.sum(jnp.where(eye, jnp.broadcast_to(row, eye.shape), 0.0), axis=1, keepdims=True)


def _srest_kernel(hq_ref, hf_ref, hi_ref, u_ref, qm_ref, st_ref, buf_ref, cm_ref, la_ref, lc_ref, oml_ref, on_ref,
                  pw_ref, ps_ref, ohg_ref, opool_ref, omem_ref, stn_ref):
    q = hq_ref[0]
    fr = hf_ref[0]
    v = hi_ref[0]
    logf = _log_decay(fr, la_ref[...], lc_ref[...])
    f = jnp.exp(logf)
    k = oml_ref[...] * _sigmoid(-fr)
    eye = (lax.broadcasted_iota(jnp.int32, (GROUP_W, GROUP_W), 0)
           == lax.broadcasted_iota(jnp.int32, (GROUP_W, GROUP_W), 1))
    qc, fc, kc = _to_col(q, eye), _to_col(f, eye), _to_col(k, eye)
    vt = jnp.concatenate([jnp.broadcast_to(v[:, h * HEAD_DIM:(h + 1) * HEAD_DIM], (HEAD_DIM, HEAD_DIM))
                          for h in range(N_HEADS)], axis=0)
    s_new = fc * st_ref[0] + kc * vt
    stn_ref[0] = s_new
    x = qc * s_new
    on = on_ref[...]
    outs = []
    for h in range(N_HEADS):
        oh = jnp.sum(x[h * HEAD_DIM:(h + 1) * HEAD_DIM], axis=0, keepdims=True)
        ms = jnp.mean(oh * oh, axis=1, keepdims=True)
        outs.append(oh * lax.rsqrt(ms + EPS) * on[:, h * HEAD_DIM:(h + 1) * HEAD_DIM])
    ohg_ref[0] = jnp.concatenate(outs, axis=1)

    u = u_ref[0]
    lane_g = _lane_head((1, GROUP_W))
    run = u
    pooled = jnp.zeros((1, GROUP_W), F32)
    nxt = 1
    for gi, wsz in enumerate(POOL_SIZES):
        for sh in range(nxt, wsz):
            run = run + buf_ref[0, POOL_BUF - sh:POOL_BUF - sh + 1, :]
        nxt = wsz
        pooled = jnp.where(lane_g == gi, run / float(wsz), pooled)
    opool_ref[0] = _hdot(pooled - u, pw_ref[...]) * ps_ref[...]

    cm = cm_ref[0]
    qrow = qm_ref[0] * QK_SCALE
    r8 = lax.broadcasted_iota(jnp.int32, (8, GROUP_W), 0)
    lh8 = _lane_head((8, GROUP_W))
    qbd = jnp.where(r8 == lh8, jnp.broadcast_to(qrow, (8, GROUP_W)), 0.0)
    s = _hdot_nt(qbd, cm[:, 0:GROUP_W])
    s = s - s.max(axis=1, keepdims=True)
    p = jnp.exp(s)
    p = p / p.sum(axis=1, keepdims=True)
    o8 = _hdot(p, cm[:, GROUP_W:2 * GROUP_W])
    omem_ref[0] = jnp.sum(jnp.where(r8 == lh8, o8, 0.0), axis=0, keepdims=True)


def _srest_call(hq, hf, hi, u, qm, state, buf, cmem, la, lc, oml, on, pw, ps, *, layer):
    nb = hq.shape[0]
    per_b = lambda rows, wd: pl.BlockSpec((1, rows, wd), lambda b: (b, 0, 0))
    per_lb = lambda rows, wd: pl.BlockSpec((1, rows, wd), lambda b: (layer * nb + b, 0, 0))
    full = lambda a: pl.BlockSpec(a.shape, lambda b: (0, 0))
    return pl.pallas_call(
        _srest_kernel,
        grid=(nb,),
        in_specs=[per_b(1, 256)] * 5 + [per_lb(GROUP_W, HEAD_DIM), per_lb(POOL_BUF, GROUP_W), per_lb(N_MEM, 2 * GROUP_W),
                                        full(la), full(lc), full(oml), full(on), full(pw), full(ps)],
        out_specs=[per_b(1, 256), per_b(1, 256), per_b(1, 256), per_b(GROUP_W, HEAD_DIM)],
        out_shape=[jax.ShapeDtypeStruct((nb, 1, 256), F32)] * 3 + [jax.ShapeDtypeStruct((nb, GROUP_W, HEAD_DIM), F32)],
        compiler_params=_cparams(("parallel",)),
    )(hq, hf, hi, u, qm, state, buf, cmem, la, lc, oml, on, pw, ps)


def _sfinal_kernel(x_ref, onsa_ref, ohg_ref, opool_ref, omem_ref, z_ref, wo_ref, y_ref):
    z = z_ref[...]
    y = jnp.concatenate([onsa_ref[...], ohg_ref[...], opool_ref[...], omem_ref[...]], axis=1) * (z * _sigmoid(z))
    y_ref[...] = x_ref[...] + _hdot(y, wo_ref[...])


def _sfinal_call(x, onsa, ohg, opool, omem, z, wo):
    full = lambda a: pl.BlockSpec(a.shape, lambda i: (0, 0))
    args = (x, onsa, ohg, opool, omem, z, wo)
    return pl.pallas_call(
        _sfinal_kernel,
        grid=(1,),
        in_specs=[full(a) for a in args],
        out_specs=full(x),
        out_shape=jax.ShapeDtypeStruct(x.shape, F32),
        compiler_params=_cparams(("arbitrary",)),
    )(*args)


def _block_diag_ones(w):
    i = np.arange(w)
    return jnp.asarray((i[:, None] // HEAD_DIM) == (i[None, :] // HEAD_DIM), BF16)


def _rope_tables(pos):
    half = HEAD_DIM // 2
    inv = ROPE_THETA ** (-jnp.arange(half, dtype=jnp.float32) / half)
    ang = pos.astype(jnp.float32)[:, None] * inv[None, :]
    cos = jnp.cos(ang)
    sin = jnp.sin(ang)
    cos64 = jnp.concatenate([cos, cos], axis=1)
    sin64 = jnp.concatenate([-sin, sin], axis=1)
    one = jnp.ones_like(cos64)
    zero = jnp.zeros_like(sin64)
    cq = jnp.tile(cos64, (1, N_HEADS))
    sq = jnp.tile(sin64, (1, N_HEADS))
    ckv = jnp.concatenate([cos64, one] * 3, axis=1)
    skv = jnp.concatenate([sin64, zero] * 3, axis=1)
    return cq, sq, ckv, skv


def _layer_params(l, norm_g, w_in, w_out, nsa_qn, nsa_kn, cmp_pe, cmp_w1, cmp_w2, lbs, hg_on, pool_w, pool_scale,
                  mem_norm, w_mem_kv, mem_qn, mem_kn):
    w = w_in[l]
    w_pad = jnp.concatenate([w[:, :652], jnp.zeros((D_MODEL, 116), F32), w[:, 652:]], axis=1)
    ones64 = jnp.ones((HEAD_DIM,), F32)
    kn = jnp.concatenate([nsa_kn[l, 0], ones64, nsa_kn[l, 1], ones64, nsa_kn[l, 2], ones64])[None, :]
    w1 = cmp_w1[l].reshape(2, L_CMP, HEAD_DIM, CMP_HID)
    w1e = jnp.zeros((L_CMP, 4, HEAD_DIM, 2 * CMP_HID), F32)
    w1e = w1e.at[:, 0, :, :CMP_HID].set(w1[0]).at[:, 1, :, CMP_HID:].set(w1[1])
    w2e = jnp.zeros((2 * CMP_HID, 128), F32)
    w2e = w2e.at[:CMP_HID, :HEAD_DIM].set(cmp_w2[l, 0]).at[CMP_HID:, HEAD_DIM:].set(cmp_w2[l, 1])
    pe = jnp.zeros((L_CMP, 4, HEAD_DIM), F32).at[:, 0].set(cmp_pe[l, 0]).at[:, 1].set(cmp_pe[l, 1])
    pw = jnp.zeros((GROUP_W, GROUP_W), F32)
    for g in range(len(POOL_SIZES)):
        pw = pw.at[g * 64:(g + 1) * 64, g * 64:(g + 1) * 64].set(pool_w[l, g])
    lb = lbs[l][None, :]
    return dict(
        g=norm_g[l][None, :], w32=w_pad, w16=w_pad.astype(BF16), wo32=w_out[l], wo16=w_out[l].astype(BF16),
        qn=jnp.tile(nsa_qn[l], N_HEADS)[None, :], kn=kn, mqn=jnp.tile(mem_qn[l], N_HEADS)[None, :],
        pe=pe.reshape(1, L_CMP * 256), w1e=w1e.reshape(L_CMP * 256, 2 * CMP_HID).astype(BF16), w2e=w2e.astype(BF16),
        la=jnp.log(lb), lc=jnp.log1p(-lb), oml=1.0 - lb, on=hg_on[l][None, :],
        pw32=pw, pw16=pw.astype(BF16), ps=pool_scale[l][None, :],
        mg=mem_norm[l][None, :], wm16=w_mem_kv[l].astype(BF16), mkn=jnp.tile(mem_kn[l], N_HEADS)[None, :],
    )


def _prompt_layer(x, mem, p, tabs, consts):
    b, t, _ = x.shape
    bd256, bd384, d16, mk = consts
    outs = _proj_call(x.reshape(b * t, D_MODEL), p['g'], p['w16'], *tabs, p['qn'], p['kn'], p['mqn'], bd256, bd384,
                      tm=512, t_tiles=t // 512, precise=False)
    q, nsa, win, gates, hq, hf, hi, u, qm, z = [o.reshape(b, t, -1) for o in outs]
    n_blk = t // L_CMP
    cc = _cmp_call(nsa.reshape(b * n_blk, L_CMP * 256), p['pe'], p['w1e'], p['w2e'],
                   tr=min(256, b * n_blk), n_rows=b * n_blk, row_off=0)
    cc = jnp.pad(cc.reshape(b, n_blk, 128), ((0, 0), (0, 128 - n_blk), (0, 0)))
    o_nsa = _nsa_call(q, nsa, win, gates, cc, tq=NSA_TQ)
    o_hg, st = _hgrn_call(hq, hf, hi, p['la'], p['lc'], p['oml'], p['on'], d16, mk, bd256)
    mkv = _memkv_call(mem, p['mg'], p['wm16'], p['mkn'], bd256)
    y = _final_call(x, o_nsa, o_hg, u, qm, z, mkv, p['pw16'], p['ps'], p['wo16'], tm=256)
    st5 = st.reshape(b, N_HEADS, HEAD_DIM, N_HEADS, HEAD_DIM)
    s_new = jnp.stack([st5[:, h, :, h, :] for h in range(N_HEADS)], axis=1)
    s_new = jnp.swapaxes(s_new, 2, 3)
    wb = min(WINDOW, t)
    return (y, nsa.reshape(b, t, 4, 1, HEAD_DIM), win[:, t - wb:].reshape(b, wb, 2, 1, HEAD_DIM), s_new,
            u[:, t - POOL_BUF:], mkv.reshape(b, N_MEM, 2, N_HEADS, HEAD_DIM))


def _sample_layer(l, x, p, tabs, consts, cache_rows, cache_blocks, win_cache, state, pool_buf, cmem, page_table,
                  n_phys):
    nb = x.shape[0]
    bd256, bd384, _, _ = consts
    outs = _proj_call(x, p['g'], p['w32'], *tabs, p['qn'], p['kn'], p['mqn'], bd256, bd384,
                      tm=nb, t_tiles=1, precise=True)
    q, nsa, win, gates, hq, hf, hi, u, qm, z = outs
    cc = _cmp_call(cache_rows, p['pe'], p['w1e'], p['w2e'], tr=256, n_rows=n_phys * 4, row_off=l * n_phys * 4)
    idx, o_c = _scmp_call(page_table, q, cc.reshape(n_phys, 512))
    r3 = lambda a: a.reshape(nb, 1, -1)
    o_nsa = _ssel_call(page_table, idx, cache_blocks, r3(q), r3(nsa), win_cache, r3(win), r3(gates),
                       o_c.reshape(nb, 8, HEAD_DIM), layer=l, n_phys=n_phys)
    o_hg, o_pool, o_mem, s_new = _srest_call(r3(hq), r3(hf), r3(hi), r3(u), r3(qm), state, pool_buf, cmem,
                                             p['la'], p['lc'], p['oml'], p['on'], p['pw32'], p['ps'], layer=l)
    y = _sfinal_call(x, o_nsa.reshape(nb, 256), o_hg.reshape(nb, 256), o_pool.reshape(nb, 256),
                     o_mem.reshape(nb, 256), z, p['wo32'])
    return y, nsa, win, s_new.reshape(nb, N_HEADS, HEAD_DIM, HEAD_DIM), u


def kernel(x_prompt, x_sample, mem_prompt, cache_nsa, cache_nsa_win, state_hgrn, state_pool, cache_mem,
           page_table, norm_g, w_in, w_out, nsa_qn, nsa_kn, cmp_pe, cmp_w1, cmp_w2, hg_lb, hg_on,
           pool_w, pool_scale, mem_norm, w_mem_kv, mem_qn, mem_kn):
    depth = w_in.shape[0]
    b, t, _ = x_prompt.shape
    nb = x_sample.shape[0]
    past_len = page_table.shape[1] * PAGE_SIZE
    n_phys = cache_nsa.shape[1]
    wb = cache_nsa_win.shape[2]

    lbs = jnp.cumsum(jax.nn.softmax(hg_lb.astype(jnp.float32), axis=0), axis=0)
    lbs = lbs - lbs[0:1]
    d_np, mk_np = _hgrn_consts()
    consts = (_block_diag_ones(256), _block_diag_ones(384), jnp.asarray(d_np, BF16), jnp.asarray(mk_np, F32))
    tabs_p = _rope_tables(jnp.arange(t))
    tabs_s = tuple(jnp.broadcast_to(a, (nb, a.shape[1])) for a in _rope_tables(past_len + jnp.arange(1)))

    cache_rows = cache_nsa.reshape(depth * n_phys * 4, L_CMP * 256)
    cache_blocks = cache_nsa.reshape(depth * n_phys * 2, L_SLC, 256)
    win_cache = cache_nsa_win.reshape(depth * nb, wb, 128)
    state = state_hgrn.reshape(depth * nb, GROUP_W, HEAD_DIM)
    pool_buf = state_pool.reshape(depth * nb, POOL_BUF, GROUP_W)
    cmem = cache_mem.reshape(depth * nb, N_MEM, 2 * GROUP_W)

    xp, xs = x_prompt, x_sample.reshape(nb, D_MODEL)
    acc = [[] for _ in range(9)]
    for l in range(depth):
        p = _layer_params(l, norm_g, w_in, w_out, nsa_qn, nsa_kn, cmp_pe, cmp_w1, cmp_w2, lbs, hg_on, pool_w,
                          pool_scale, mem_norm, w_mem_kv, mem_qn, mem_kn)
        xp, a, bw, c, d, e = _prompt_layer(xp, mem_prompt, p, tabs_p, consts)
        xs, sa, sw, sc, sd = _sample_layer(l, xs, p, tabs_s, consts, cache_rows, cache_blocks, win_cache, state,
                                           pool_buf, cmem, page_table, n_phys)
        new_win_s = jnp.concatenate([cache_nsa_win[l][:, 1:], sw.reshape(nb, 1, 2, 1, HEAD_DIM)], axis=1)[:, -wb:]
        new_pool_s = jnp.concatenate([state_pool[l][:, 1:], sd.reshape(nb, 1, GROUP_W)], axis=1)
        for lst, val in zip(acc, (a, bw, c, d, e, sa.reshape(nb, 1, 4, 1, HEAD_DIM), new_win_s, sc, new_pool_s)):
            lst.append(val)
    return (xp, xs.reshape(nb, 1, D_MODEL)) + tuple(jnp.stack(v) for v in acc)
```

```python
import functools

import numpy as np
import jax
import jax.numpy as jnp
from jax import lax
from jax.experimental import pallas as pl
from jax.experimental.pallas import tpu as pltpu

D_MODEL = 1024
HEAD_DIM = 64
GROUP_W = 256
N_HEADS = 4
L_CMP = 32
L_SLC = 64
N_SEL = 16
WINDOW = 512
CMP_HID = 256
POOL_SIZES = (2, 4, 8, 16)
POOL_BUF = 15
N_MEM = 256
PAGE_SIZE = 128
ROPE_THETA = 10000.0
EPS = 1e-6
QK_SCALE = HEAD_DIM ** -0.5

PROJ_W = 3072
HG_CHUNK = 256
NSA_TQ = 256
NEG = -1e30
VMEM_LIMIT = 56 * 1024 * 1024

F32 = jnp.float32
BF16 = jnp.bfloat16
HI = lax.Precision.HIGHEST


def _cparams(sem):
    return pltpu.CompilerParams(dimension_semantics=sem, vmem_limit_bytes=VMEM_LIMIT)


def _bdot(a, b):
    return jnp.dot(a.astype(BF16), b.astype(BF16), preferred_element_type=F32)


def _bdot_nt(a, b):
    return lax.dot_general(a.astype(BF16), b.astype(BF16), (((1,), (1,)), ((), ())), preferred_element_type=F32)


def _hdot(a, b):
    return jnp.dot(a, b, precision=HI, preferred_element_type=F32)


def _hdot_nt(a, b):
    return lax.dot_general(a, b, (((1,), (1,)), ((), ())), precision=HI, preferred_element_type=F32)


def _dot2(a, b16):
    hi = a.astype(BF16)
    lo = (a - hi.astype(F32)).astype(BF16)
    return jnp.dot(hi, b16, preferred_element_type=F32) + jnp.dot(lo, b16, preferred_element_type=F32)


def _dot2_left(a16, b):
    hi = b.astype(BF16)
    lo = (b - hi.astype(F32)).astype(BF16)
    return jnp.dot(a16, hi, preferred_element_type=F32) + jnp.dot(a16, lo, preferred_element_type=F32)


def _seg_rms(x, bd16, gain):
    ms = _dot2(x * x, bd16) * (1.0 / HEAD_DIM)
    return x * lax.rsqrt(ms + EPS) * gain


def _rope(x, cos, sin_signed):
    w = x.shape[1]
    lane = lax.broadcasted_iota(jnp.int32, x.shape, 1)
    first = (lane & 32) == 0
    rot = jnp.where(first, pltpu.roll(x, w - 32, 1), pltpu.roll(x, 32, 1))
    return x * cos + rot * sin_signed


def _sigmoid(x):
    return jax.nn.sigmoid(x)


def _lane_head(shape):
    return lax.broadcasted_iota(jnp.int32, shape, 1) // HEAD_DIM


def _proj_kernel(x_ref, g_ref, w_ref, cq_ref, sq_ref, ckv_ref, skv_ref, qn_ref, kn_ref, mqn_ref, bd256_ref, bd384_ref,
                 q_out, nsa_out, win_out, gate_out, hq_out, hf_out, hi_out, u_out, qm_out, z_out, *, precise):
    x = x_ref[...]
    ms = jnp.mean(x * x, axis=-1, keepdims=True)
    h = x * lax.rsqrt(ms + EPS) * g_ref[...]
    proj = _hdot(h, w_ref[...]) if precise else _bdot(h, w_ref[...])
    bd256 = bd256_ref[...]
    q = _rope(_seg_rms(proj[:, 0:256], bd256, qn_ref[...]), cq_ref[...], sq_ref[...])
    q_out[...] = q
    kv = proj[:, 256:640]
    kvn = _rope(_seg_rms(kv, bd384_ref[...], kn_ref[...]), ckv_ref[...], skv_ref[...])
    lane = lax.broadcasted_iota(jnp.int32, kv.shape, 1)
    is_key = ((lane // HEAD_DIM) & 1) == 0
    kv = jnp.where(is_key, kvn, kv)
    nsa_out[...] = kv[:, 0:256]
    win_out[...] = kv[:, 256:384]
    gate_out[...] = _sigmoid(proj[:, 640:768])
    hq_out[...] = proj[:, 768:1024]
    hf_out[...] = proj[:, 1024:1280]
    hi_out[...] = proj[:, 1280:1536]
    u_out[...] = proj[:, 1536:1792]
    qm_out[...] = _seg_rms(proj[:, 1792:2048], bd256, mqn_ref[...])
    z_out[...] = proj[:, 2048:3072]


def _proj_call(x, g, w, cq, sq, ckv, skv, qn, kn, mqn, bd256, bd384, *, tm, t_tiles, precise):
    n = x.shape[0]
    row = lambda wd: pl.BlockSpec((tm, wd), lambda i: (i, 0))
    tab = lambda wd: pl.BlockSpec((tm, wd), lambda i: (i % t_tiles, 0))
    full = lambda a: pl.BlockSpec(a.shape, lambda i: (0, 0))
    widths = (256, 256, 128, 128, 256, 256, 256, 256, 256, 1024)
    return pl.pallas_call(
        functools.partial(_proj_kernel, precise=precise),
        grid=(n // tm,),
        in_specs=[row(D_MODEL), full(g), full(w), tab(256), tab(256), tab(384), tab(384),
                  full(qn), full(kn), full(mqn), full(bd256), full(bd384)],
        out_specs=[row(wd) for wd in widths],
        out_shape=[jax.ShapeDtypeStruct((n, wd), F32) for wd in widths],
        compiler_params=_cparams(("parallel",)),
    )(x, g, w, cq, sq, ckv, skv, qn, kn, mqn, bd256, bd384)


def _cmp_kernel(x_ref, pe_ref, w1_ref, w2_ref, o_ref):
    x = x_ref[...] + pe_ref[...]
    h = _bdot(x, w1_ref[...])
    h = h * _sigmoid(h)
    o_ref[...] = _bdot(h, w2_ref[...])


def _cmp_call(x, pe, w1, w2, *, tr, n_rows, row_off):
    off = row_off // tr
    full = lambda a: pl.BlockSpec(a.shape, lambda i: (0, 0))
    return pl.pallas_call(
        _cmp_kernel,
        grid=(n_rows // tr,),
        in_specs=[pl.BlockSpec((tr, x.shape[1]), lambda i: (i + off, 0)), full(pe), full(w1), full(w2)],
        out_specs=pl.BlockSpec((tr, 128), lambda i: (i, 0)),
        out_shape=jax.ShapeDtypeStruct((n_rows, 128), F32),
        compiler_params=_cparams(("parallel",)),
    )(x, pe, w1, w2)


def _cmp_pages_kernel(pg_ref, pe_ref, wr_ref, w2_ref, o_ref, xs_scr):
    npg = pg_ref.shape[0]

    def xpose(p, c):
        xs_scr[pl.ds(pl.multiple_of(p * PAGE_SIZE, PAGE_SIZE), PAGE_SIZE), :] = pg_ref[p].T
        return c

    lax.fori_loop(0, npg, xpose, 0)
    n_out = npg * (PAGE_SIZE // L_CMP)
    acc = jnp.zeros((n_out, 2 * CMP_HID), F32)
    for r in range(L_CMP):
        slab = xs_scr[pl.ds(r, n_out, stride=L_CMP), :] + pe_ref[r:r + 1, :]
        acc = acc + _bdot(slab, wr_ref[r])
    h = acc * _sigmoid(acc)
    o_ref[...] = _bdot(h, w2_ref[...])


def _cmp_pages_call(pages, pe, wr, w2, *, pp, n_pages, page_off):
    off = page_off // pp
    n_out = pp * (PAGE_SIZE // L_CMP)
    return pl.pallas_call(
        _cmp_pages_kernel,
        grid=(n_pages // pp,),
        in_specs=[pl.BlockSpec((pp, 128, PAGE_SIZE), lambda i: (i + off, 0, 0)),
                  pl.BlockSpec(pe.shape, lambda i: (0, 0)),
                  pl.BlockSpec(wr.shape, lambda i: (0, 0, 0)),
                  pl.BlockSpec(w2.shape, lambda i: (0, 0))],
        out_specs=pl.BlockSpec((n_out, 128), lambda i: (i, 0)),
        out_shape=jax.ShapeDtypeStruct((n_pages * (PAGE_SIZE // L_CMP), 128), F32),
        scratch_shapes=[pltpu.VMEM((pp * PAGE_SIZE, 128), F32)],
        compiler_params=_cparams(("parallel",)),
    )(pages, pe, wr, w2)


def _topk_mask(score, k, idx_f):
    sel = jnp.zeros(score.shape, jnp.bool_)
    s = score
    big = jnp.float32(1e9)
    picks = []
    for _ in range(k):
        m = jnp.max(s, axis=1, keepdims=True)
        cand = jnp.where(s == m, idx_f, big)
        pick = jnp.min(cand, axis=1, keepdims=True)
        hit = idx_f == pick
        sel = sel | hit
        s = jnp.where(hit, -jnp.inf, s)
        picks.append(pick)
    return sel, picks


def _softmax_update(s, mask, m, l, acc, v):
    s = jnp.where(mask, s, NEG)
    m_new = jnp.maximum(m, jnp.max(s, axis=1, keepdims=True))
    alpha = jnp.exp(m - m_new)
    p = jnp.where(mask, jnp.exp(s - m_new), 0.0)
    l = alpha * l + jnp.sum(p, axis=1, keepdims=True)
    acc = alpha * acc + _bdot(p, v)
    return m_new, l, acc


def _nsa_kernel(q_ref, nsa_ref, win_ref, gate_ref, cc_ref, o_ref, qs_ref, *, tq):
    i = pl.program_id(1)
    t0 = i * tq
    r4 = N_HEADS * tq
    q = q_ref[0] * QK_SCALE
    for h in range(N_HEADS):
        qs_ref[h * tq:(h + 1) * tq, :] = q[:, h * HEAD_DIM:(h + 1) * HEAD_DIM].astype(BF16)
    qs = qs_ref[...]
    trow = t0 + lax.broadcasted_iota(jnp.int32, (r4, 1), 0) % tq

    cc = cc_ref[0]
    s_c = _bdot_nt(qs, cc[:, 0:HEAD_DIM])
    n_idx = lax.broadcasted_iota(jnp.int32, s_c.shape, 1)
    done = (n_idx + 1) * L_CMP - 1 <= trow
    s_m = jnp.where(done, s_c, NEG)
    m_c = jnp.max(s_m, axis=1, keepdims=True)
    m_c = jnp.where(m_c < 0.5 * NEG, 0.0, m_c)
    p_c = jnp.where(done, jnp.exp(s_m - m_c), 0.0)
    p_c = p_c / jnp.maximum(jnp.sum(p_c, axis=1, keepdims=True), 1e-30)
    o_c = _bdot(p_c, cc[:, HEAD_DIM:2 * HEAD_DIM])

    imp = p_c[0:tq] + p_c[tq:2 * tq] + p_c[2 * tq:3 * tq] + p_c[3 * tq:4 * tq]
    imp2 = imp + pltpu.roll(imp, 127, 1)
    lane = lax.broadcasted_iota(jnp.int32, imp.shape, 1)
    tq_pos = t0 + lax.broadcasted_iota(jnp.int32, (tq, 1), 0)
    blk_t = tq_pos // L_SLC
    j = lane // 2
    n_slc = nsa_ref.shape[1] // L_SLC
    valid = ((lane & 1) == 0) & (j < n_slc)
    avail = valid & (j <= blk_t)
    forced = (j == 0) | (j == blk_t) | (j == blk_t - 1)
    score = jnp.where(avail, jnp.where(forced, jnp.inf, imp2), -jnp.inf)
    score = jnp.where(valid, score, -jnp.inf)
    sel, _ = _topk_mask(score, N_SEL, lane.astype(F32))
    sel16 = jnp.where(sel & avail, 1.0, 0.0).astype(BF16)

    zero_m = jnp.full((r4, 1), NEG, F32)
    zero_l = jnp.zeros((r4, 1), F32)
    zero_a = jnp.zeros((r4, HEAD_DIM), F32)

    def kpos_of(c):
        return c * tq + lax.broadcasted_iota(jnp.int32, (r4, tq), 1)

    def sel_body(c, carry):
        m, l, acc = carry
        start = pl.multiple_of(c * tq, tq)
        kv = nsa_ref[0, pl.ds(start, tq), :]
        s = _bdot_nt(qs, kv[:, 128:192])
        el = lax.broadcasted_iota(jnp.int32, (128, tq), 0)
        ek = c * tq + lax.broadcasted_iota(jnp.int32, (128, tq), 1)
        expand = jnp.where(el == 2 * (ek // L_SLC), 1.0, 0.0).astype(BF16)
        msel = jnp.dot(sel16, expand, preferred_element_type=F32)
        msel = jnp.concatenate([msel] * N_HEADS, axis=0)
        mask = (msel > 0.5) & (kpos_of(c) <= trow)
        return _softmax_update(s, mask, m, l, acc, kv[:, 192:256])

    m_s, l_s, a_s = lax.fori_loop(0, i + 1, sel_body, (zero_m, zero_l, zero_a))
    o_s = a_s / jnp.maximum(l_s, 1e-30)

    def win_body(c, carry):
        m, l, acc = carry
        start = pl.multiple_of(c * tq, tq)
        kv = win_ref[0, pl.ds(start, tq), :]
        s = _bdot_nt(qs, kv[:, 0:HEAD_DIM])
        kpos = kpos_of(c)
        mask = (kpos <= trow) & (kpos > trow - WINDOW)
        return _softmax_update(s, mask, m, l, acc, kv[:, HEAD_DIM:2 * HEAD_DIM])

    c_lo = jnp.maximum(i - WINDOW // tq, 0)
    m_w, l_w, a_w = lax.fori_loop(c_lo, i + 1, win_body, (zero_m, zero_l, zero_a))
    o_w = a_w / jnp.maximum(l_w, 1e-30)

    g = gate_ref[0]
    outs = []
    for h in range(N_HEADS):
        rows = slice(h * tq, (h + 1) * tq)
        outs.append(g[:, 3 * h:3 * h + 1] * o_c[rows] + g[:, 3 * h + 1:3 * h + 2] * o_s[rows]
                    + g[:, 3 * h + 2:3 * h + 3] * o_w[rows])
    o_ref[0] = jnp.concatenate(outs, axis=1)


def _nsa_call(q, nsa, win, gates, cc, *, tq):
    b, t, _ = q.shape
    return pl.pallas_call(
        functools.partial(_nsa_kernel, tq=tq),
        grid=(b, t // tq),
        in_specs=[pl.BlockSpec((1, tq, 256), lambda bi, i: (bi, i, 0)),
                  pl.BlockSpec((1, t, 256), lambda bi, i: (bi, 0, 0)),
                  pl.BlockSpec((1, t, 128), lambda bi, i: (bi, 0, 0)),
                  pl.BlockSpec((1, tq, 128), lambda bi, i: (bi, i, 0)),
                  pl.BlockSpec((1, 128, 128), lambda bi, i: (bi, 0, 0))],
        out_specs=pl.BlockSpec((1, tq, 256), lambda bi, i: (bi, i, 0)),
        out_shape=jax.ShapeDtypeStruct((b, t, 256), F32),
        scratch_shapes=[pltpu.VMEM((N_HEADS * tq, HEAD_DIM), BF16)],
        compiler_params=_cparams(("parallel", "arbitrary")),
    )(q, nsa, win, gates, cc)


def _hgrn_consts(c=HG_CHUNK):
    t = np.arange(c)[:, None]
    k = np.arange(c)[None, :]
    blocks = [k <= t, k > t]
    masks = []
    m = c // 2
    while m >= 1:
        blk = t // (2 * m)
        upper = (t % (2 * m)) >= m
        ref = blk * 2 * m + m - 1
        blocks.append(upper & (k > ref) & (k <= t))
        blocks.append((~upper) & (k > t) & (k <= ref))
        masks.append(upper & ((k // (2 * m)) == blk) & ((k % (2 * m)) < m))
        m //= 2
    blocks.append(np.zeros((c, c), bool))
    blocks.append(np.zeros((c, c), bool))
    masks.append(t == k)
    d = np.concatenate(blocks, 0).astype(np.float32)
    mk = np.stack(masks).astype(np.float32)
    return d, mk


def _log_decay(fr, la, lc):
    ls = jnp.minimum(fr, 0.0) - jnp.log1p(jnp.exp(-jnp.abs(fr)))
    b = lc + ls
    return jnp.maximum(la, b) + jnp.log1p(jnp.exp(-jnp.abs(la - b)))


def _hgrn_kernel(hq_ref, hf_ref, hi_ref, la_ref, lc_ref, oml_ref, on_ref, d_ref, mk_ref, bd_ref,
                 o_ref, st_ref, st_scr):
    c = HG_CHUNK
    jt = pl.program_id(1)

    @pl.when(jt == 0)
    def _():
        st_scr[...] = jnp.zeros_like(st_scr)

    q = hq_ref[0]
    fr = hf_ref[0]
    v = hi_ref[0]
    logf = _log_decay(fr, la_ref[...], lc_ref[...])
    k = oml_ref[...] * _sigmoid(-fr)
    e_all = _dot2_left(d_ref[...], logf)
    gc = e_all[0:c]
    esuf = e_all[c:2 * c]
    lane_h = _lane_head((c, GROUP_W))
    n_lvl = mk_ref.shape[0]
    a_tot = [jnp.zeros((c, c), F32) for _ in range(N_HEADS)]
    for lv in range(n_lvl):
        eq = e_all[(2 + 2 * lv) * c:(3 + 2 * lv) * c]
        ek = e_all[(3 + 2 * lv) * c:(4 + 2 * lv) * c]
        qt = q * jnp.exp(eq)
        kt = (k * jnp.exp(ek)).astype(BF16)
        qs = jnp.concatenate([jnp.where(lane_h == h, qt, 0.0).astype(BF16) for h in range(N_HEADS)], axis=0)
        a_l = lax.dot_general(qs, kt, (((1,), (1,)), ((), ())), preferred_element_type=F32)
        mk = mk_ref[lv]
        for h in range(N_HEADS):
            a_tot[h] = a_tot[h] + mk * a_l[h * c:(h + 1) * c]
    a_all = jnp.concatenate(a_tot, axis=0)
    r = _bdot(a_all, v)
    o = jnp.zeros((c, GROUP_W), F32)
    for h in range(N_HEADS):
        o = o + jnp.where(lane_h == h, r[h * c:(h + 1) * c], 0.0)
    st = st_scr[...]
    o = o + _bdot_nt(q * jnp.exp(gc), st)
    gl = gc[c - 1:c, :]
    khat = k * jnp.exp(esuf)
    upd = jnp.dot(v.T.astype(BF16), khat.astype(BF16), preferred_element_type=F32)
    row_h = lax.broadcasted_iota(jnp.int32, (GROUP_W, GROUP_W), 0) // HEAD_DIM
    col_h = _lane_head((GROUP_W, GROUP_W))
    st_new = jnp.exp(gl) * st + jnp.where(row_h == col_h, upd, 0.0)
    st_scr[...] = st_new
    st_ref[0] = st_new
    o_ref[0] = _seg_rms(o, bd_ref[...], on_ref[...])


def _hgrn_call(hq, hf, hi, la, lc, oml, on, d16, mk, bd256):
    b, t, _ = hq.shape
    c = HG_CHUNK
    row = pl.BlockSpec((1, c, GROUP_W), lambda bi, j: (bi, j, 0))
    full2 = lambda a: pl.BlockSpec(a.shape, lambda bi, j: (0, 0))
    return pl.pallas_call(
        _hgrn_kernel,
        grid=(b, t // c),
        in_specs=[row, row, row, full2(la), full2(lc), full2(oml), full2(on), full2(d16),
                  pl.BlockSpec(mk.shape, lambda bi, j: (0, 0, 0)), full2(bd256)],
        out_specs=[row, pl.BlockSpec((1, GROUP_W, GROUP_W), lambda bi, j: (bi, 0, 0))],
        out_shape=[jax.ShapeDtypeStruct((b, t, GROUP_W), F32), jax.ShapeDtypeStruct((b, GROUP_W, GROUP_W), F32)],
        scratch_shapes=[pltpu.VMEM((GROUP_W, GROUP_W), F32)],
        compiler_params=_cparams(("parallel", "arbitrary")),
    )(hq, hf, hi, la, lc, oml, on, d16, mk, bd256)


def _memkv_kernel(m_ref, g_ref, w_ref, kn_ref, bd_ref, o_ref):
    x = m_ref[0]
    ms = jnp.mean(x * x, axis=-1, keepdims=True)
    h = x * lax.rsqrt(ms + EPS) * g_ref[...]
    kv = _bdot(h, w_ref[...])
    o_ref[0, :, 0:GROUP_W] = _seg_rms(kv[:, 0:GROUP_W], bd_ref[...], kn_ref[...])
    o_ref[0, :, GROUP_W:2 * GROUP_W] = kv[:, GROUP_W:2 * GROUP_W]


def _memkv_call(mem, g, w, kn, bd256):
    b = mem.shape[0]
    full = lambda a: pl.BlockSpec(a.shape, lambda i: (0, 0))
    return pl.pallas_call(
        _memkv_kernel,
        grid=(b,),
        in_specs=[pl.BlockSpec((1, N_MEM, D_MODEL), lambda i: (i, 0, 0)), full(g), full(w), full(kn), full(bd256)],
        out_specs=pl.BlockSpec((1, N_MEM, 2 * GROUP_W), lambda i: (i, 0, 0)),
        out_shape=jax.ShapeDtypeStruct((b, N_MEM, 2 * GROUP_W), F32),
        compiler_params=_cparams(("parallel",)),
    )(mem, g, w, kn, bd256)


def _final_kernel(x_ref, onsa_ref, ohg_ref, u_ref, halo_ref, qm_ref, z_ref, mkv_ref, pw_ref, ps_ref, wo_ref,
                  y_ref, ext_scr, *, tm):
    jt = pl.program_id(1)
    u = u_ref[0]
    halo = halo_ref[0]
    ext_scr[0:16, :] = jnp.where(jt == 0, 0.0, halo)
    ext_scr[16:16 + tm, :] = u
    pos = jt * tm + lax.broadcasted_iota(jnp.int32, (tm, 1), 0)
    lane_g = _lane_head((tm, GROUP_W))
    run = u
    pooled = jnp.zeros((tm, GROUP_W), F32)
    nxt = 1
    for gi, wsz in enumerate(POOL_SIZES):
        for sh in range(nxt, wsz):
            run = run + ext_scr[16 - sh:16 - sh + tm, :]
        nxt = wsz
        cnt = jnp.minimum(wsz, pos + 1).astype(F32)
        pooled = jnp.where(lane_g == gi, run / cnt, pooled)
    o_pool = _bdot(pooled - u, pw_ref[...]) * ps_ref[...]

    qm = qm_ref[0] * QK_SCALE
    mkv = mkv_ref[0]
    heads = []
    for h in range(N_HEADS):
        sl = slice(h * HEAD_DIM, (h + 1) * HEAD_DIM)
        s = _bdot_nt(qm[:, sl], mkv[:, sl])
        s = s - jnp.max(s, axis=1, keepdims=True)
        p = jnp.exp(s)
        p = p / jnp.sum(p, axis=1, keepdims=True)
        heads.append(_bdot(p, mkv[:, GROUP_W + h * HEAD_DIM:GROUP_W + (h + 1) * HEAD_DIM]))
    o_mem = jnp.concatenate(heads, axis=1)

    z = z_ref[0]
    y = jnp.concatenate([onsa_ref[0], ohg_ref[0], o_pool, o_mem], axis=1) * (z * _sigmoid(z))
    y_ref[0] = x_ref[0] + _bdot(y, wo_ref[...])


def _final_call(x, onsa, ohg, u, qm, z, mkv, pw, ps, wo, *, tm):
    b, t, _ = x.shape
    row = lambda wd: pl.BlockSpec((1, tm, wd), lambda bi, j: (bi, j, 0))
    full2 = lambda a: pl.BlockSpec(a.shape, lambda bi, j: (0, 0))
    hb = tm // 16
    return pl.pallas_call(
        functools.partial(_final_kernel, tm=tm),
        grid=(b, t // tm),
        in_specs=[row(D_MODEL), row(GROUP_W), row(GROUP_W), row(GROUP_W),
                  pl.BlockSpec((1, 16, GROUP_W), lambda bi, j: (bi, jnp.maximum(j * hb - 1, 0), 0)),
                  row(GROUP_W), row(D_MODEL),
                  pl.BlockSpec((1, N_MEM, 2 * GROUP_W), lambda bi, j: (bi, 0, 0)),
                  full2(pw), full2(ps), full2(wo)],
        out_specs=row(D_MODEL),
        out_shape=jax.ShapeDtypeStruct((b, t, D_MODEL), F32),
        scratch_shapes=[pltpu.VMEM((tm + 16, GROUP_W), F32)],
        compiler_params=_cparams(("parallel", "arbitrary")),
    )(x, onsa, ohg, u, u, qm, z, mkv, pw, ps, wo)


def _head_rows(row):
    r8 = lax.broadcasted_iota(jnp.int32, (8, GROUP_W), 0)
    qm = jnp.where(r8 == _lane_head((8, GROUP_W)), jnp.broadcast_to(row, (8, GROUP_W)), 0.0)
    return qm[:, 0:64] + qm[:, 64:128] + qm[:, 128:192] + qm[:, 192:256]


def _scmp_kernel(pt_ref, q_ref, cc_hbm, idx_ref, oc_ref, g_scr, a_scr, b_scr, sem, *, nb, npg):
    def copy(n):
        b = n // npg
        pg = pt_ref[b, n - b * npg]
        return pltpu.make_async_copy(cc_hbm.at[pl.ds(pg, 1)], g_scr.at[pl.ds(n, 1)], sem)

    def start(n, c):
        copy(n).start()
        return c

    def wait(n, c):
        copy(n).wait()
        return c

    lax.fori_loop(0, nb * npg, start, 0)
    lax.fori_loop(0, nb * npg, wait, 0)

    def per_batch(b, c):
        gb = g_scr[pl.ds(pl.multiple_of(b * npg, npg), npg), :]
        qh = _head_rows(q_ref[pl.ds(b, 1), :]) * QK_SCALE
        ss = [_hdot_nt(qh, gb[:, r * 128:r * 128 + HEAD_DIM]) for r in range(4)]
        m = ss[0].max(axis=1, keepdims=True)
        for r in range(1, 4):
            m = jnp.maximum(m, ss[r].max(axis=1, keepdims=True))
        ps = [jnp.exp(s - m) for s in ss]
        l = ps[0].sum(axis=1, keepdims=True)
        for r in range(1, 4):
            l = l + ps[r].sum(axis=1, keepdims=True)
        inv = 1.0 / jnp.maximum(l, 1e-30)
        ps = [p * inv for p in ps]
        oc = _hdot(ps[0], gb[:, HEAD_DIM:128])
        for r in range(1, 4):
            oc = oc + _hdot(ps[r], gb[:, r * 128 + HEAD_DIM:(r + 1) * 128])
        oc_ref[pl.ds(pl.multiple_of(b * 8, 8), 8), :] = oc
        hmask = lax.broadcasted_iota(jnp.int32, ss[0].shape, 0) < N_HEADS
        imps = [jnp.sum(jnp.where(hmask, p, 0.0), axis=0, keepdims=True) for p in ps]
        a_scr[pl.ds(b, 1), :] = imps[0] + imps[1]
        b_scr[pl.ds(b, 1), :] = imps[2] + imps[3]
        return c

    lax.fori_loop(0, nb, per_batch, 0)

    score = jnp.concatenate([a_scr[...], b_scr[...]], axis=1)
    lane = lax.broadcasted_iota(jnp.int32, score.shape, 1)
    jidx = jnp.where(lane < npg, 2 * lane, 2 * (lane - npg) + 1)
    forced = (jidx == 0) | (jidx == 2 * npg - 1)
    score = jnp.where(forced, jnp.inf, score)
    _, picks = _topk_mask(score, N_SEL - 1, jidx.astype(F32))
    out_lane = lax.broadcasted_iota(jnp.int32, (nb, 128), 1)
    out = jnp.zeros((nb, 128), jnp.int32)
    for r, pk in enumerate(picks):
        out = jnp.where(out_lane == r, pk.astype(jnp.int32), out)
    idx_ref[...] = out


def _scmp_call(page_table, q, cc_rows):
    nb, npg = page_table.shape
    return pl.pallas_call(
        functools.partial(_scmp_kernel, nb=nb, npg=npg),
        grid_spec=pltpu.PrefetchScalarGridSpec(
            num_scalar_prefetch=1,
            grid=(1,),
            in_specs=[pl.BlockSpec(q.shape, lambda i, pt: (0, 0)), pl.BlockSpec(memory_space=pl.ANY)],
            out_specs=[pl.BlockSpec((nb, 128), lambda i, pt: (0, 0)), pl.BlockSpec((nb * 8, HEAD_DIM), lambda i, pt: (0, 0))],
            scratch_shapes=[pltpu.VMEM((nb * npg, 512), F32), pltpu.VMEM((nb, 128), F32), pltpu.VMEM((nb, 128), F32),
                            pltpu.SemaphoreType.DMA(())],
        ),
        out_shape=[jax.ShapeDtypeStruct((nb, 128), jnp.int32), jax.ShapeDtypeStruct((nb * 8, HEAD_DIM), F32)],
        compiler_params=_cparams(("arbitrary",)),
    )(page_table, q, cc_rows)


def _ssel_kernel(pt_ref, idx_ref, blk_ref, q_ref, new_ref, wc_ref, wnew_ref, gate_ref, oc_ref, o_ref,
                 m_scr, l_scr, a_scr, *, n_gather):
    r = pl.program_id(1)

    @pl.when(r == 0)
    def _():
        m_scr[...] = jnp.full_like(m_scr, NEG)
        l_scr[...] = jnp.zeros_like(l_scr)
        a_scr[...] = jnp.zeros_like(a_scr)

    qh = _head_rows(q_ref[0]) * QK_SCALE
    blk = blk_ref[0]
    half = idx_ref[pl.program_id(0), r] % 2
    s = _hdot(qh, blk[0:HEAD_DIM])
    mine = lax.broadcasted_iota(jnp.int32, s.shape, 1) // L_SLC == half
    s = jnp.where(mine, s, NEG)
    m_old = m_scr[...]
    m_new = jnp.maximum(m_old, s.max(axis=1, keepdims=True))
    alpha = jnp.exp(m_old - m_new)
    p = jnp.where(mine, jnp.exp(s - m_new), 0.0)
    l_scr[...] = alpha * l_scr[...] + p.sum(axis=1, keepdims=True)
    a_scr[...] = alpha * a_scr[...] + _hdot_nt(p, blk[HEAD_DIM:2 * HEAD_DIM])
    m_scr[...] = m_new

    @pl.when(r == n_gather - 1)
    def _():
        new = new_ref[0]
        s_n = jnp.sum(qh * new[:, 128:192], axis=1, keepdims=True)
        m0 = m_scr[...]
        m1 = jnp.maximum(m0, s_n)
        al = jnp.exp(m0 - m1)
        p_n = jnp.exp(s_n - m1)
        l1 = al * l_scr[...] + p_n
        o_s = (al * a_scr[...] + p_n * new[:, 192:256]) / jnp.maximum(l1, 1e-30)
        wc = wc_ref[0]
        wnew = wnew_ref[0]
        s_w = _hdot(qh, wc[0:HEAD_DIM])
        keep = lax.broadcasted_iota(jnp.int32, s_w.shape, 1) >= wc.shape[1] + 1 - WINDOW
        s_w = jnp.where(keep, s_w, NEG)
        s_wn = jnp.sum(qh * wnew[:, 0:HEAD_DIM], axis=1, keepdims=True)
        m_w = jnp.maximum(s_w.max(axis=1, keepdims=True), s_wn)
        p_w = jnp.where(keep, jnp.exp(s_w - m_w), 0.0)
        p_wn = jnp.exp(s_wn - m_w)
        l_w = p_w.sum(axis=1, keepdims=True) + p_wn
        o_w = (_hdot_nt(p_w, wc[HEAD_DIM:128]) + p_wn * wnew[:, HEAD_DIM:128]) / jnp.maximum(l_w, 1e-30)
        o_c = oc_ref[0]
        g = gate_ref[0]
        outs = []
        for h in range(N_HEADS):
            outs.append(g[:, 3 * h:3 * h + 1] * o_c[h:h + 1] + g[:, 3 * h + 1:3 * h + 2] * o_s[h:h + 1]
                        + g[:, 3 * h + 2:3 * h + 3] * o_w[h:h + 1])
        o_ref[0] = jnp.concatenate(outs, axis=1)


def _ssel_call(page_table, idx, cache_blocks, q, new_rows, win_cache, win_new, gates, o_c, *, layer, n_phys):
    nb = q.shape[0]
    n_gather = N_SEL - 1
    wb = win_cache.shape[2]
    pg_off = layer * n_phys

    def blk_map(b, r, pt, ix):
        return ((pg_off + pt[b, ix[b, r] // 2]) * 2 + 1, 0, 0)

    per_b = lambda rows, wd: pl.BlockSpec((1, rows, wd), lambda b, r, pt, ix: (b, 0, 0))
    return pl.pallas_call(
        functools.partial(_ssel_kernel, n_gather=n_gather),
        grid_spec=pltpu.PrefetchScalarGridSpec(
            num_scalar_prefetch=2,
            grid=(nb, n_gather),
            in_specs=[pl.BlockSpec((1, 128, PAGE_SIZE), blk_map), per_b(1, 256), per_b(1, 256),
                      pl.BlockSpec((1, 128, wb), lambda b, r, pt, ix: (layer * nb + b, 0, 0)),
                      per_b(1, 128), per_b(1, 128), per_b(8, HEAD_DIM)],
            out_specs=per_b(1, 256),
            scratch_shapes=[pltpu.VMEM((8, 1), F32), pltpu.VMEM((8, 1), F32), pltpu.VMEM((8, HEAD_DIM), F32)],
        ),
        out_shape=jax.ShapeDtypeStruct((nb, 1, 256), F32),
        compiler_params=_cparams(("arbitrary", "arbitrary")),
    )(page_table, idx, cache_blocks, q, new_rows, win_cache, win_new, gates, o_c)


def _to_col(row, eye):
    return jnp.sum(jnp.where(eye, jnp.broadcast_to(row, eye.shape), 0.0), axis=1, keepdims=True)


def _srest_kernel(hq_ref, hf_ref, hi_ref, qm_ref, st_ref, cm_ref, la_ref, lc_ref, oml_ref, on_ref,
                  ohg_ref, omem_ref, stn_ref):
    q = hq_ref[0]
    fr = hf_ref[0]
    v = hi_ref[0]
    logf = _log_decay(fr, la_ref[...], lc_ref[...])
    f = jnp.exp(logf)
    k = oml_ref[...] * _sigmoid(-fr)
    eye = (lax.broadcasted_iota(jnp.int32, (GROUP_W, GROUP_W), 0)
           == lax.broadcasted_iota(jnp.int32, (GROUP_W, GROUP_W), 1))
    qc, fc, kc = _to_col(q, eye), _to_col(f, eye), _to_col(k, eye)
    vt = jnp.concatenate([jnp.broadcast_to(v[:, h * HEAD_DIM:(h + 1) * HEAD_DIM], (HEAD_DIM, HEAD_DIM))
                          for h in range(N_HEADS)], axis=0)
    s_new = fc * st_ref[0] + kc * vt
    stn_ref[0] = s_new
    x = qc * s_new
    on = on_ref[...]
    outs = []
    for h in range(N_HEADS):
        oh = jnp.sum(x[h * HEAD_DIM:(h + 1) * HEAD_DIM], axis=0, keepdims=True)
        ms = jnp.mean(oh * oh, axis=1, keepdims=True)
        outs.append(oh * lax.rsqrt(ms + EPS) * on[:, h * HEAD_DIM:(h + 1) * HEAD_DIM])
    ohg_ref[0] = jnp.concatenate(outs, axis=1)

    cm = cm_ref[0]
    qrow = qm_ref[0] * QK_SCALE
    r8 = lax.broadcasted_iota(jnp.int32, (8, GROUP_W), 0)
    lh8 = _lane_head((8, GROUP_W))
    qbd = jnp.where(r8 == lh8, jnp.broadcast_to(qrow, (8, GROUP_W)), 0.0)
    s = _hdot(qbd, cm[0:GROUP_W])
    s = s - s.max(axis=1, keepdims=True)
    p = jnp.exp(s)
    p = p / p.sum(axis=1, keepdims=True)
    o8 = _hdot_nt(p, cm[GROUP_W:2 * GROUP_W])
    omem_ref[0] = jnp.sum(jnp.where(r8 == lh8, o8, 0.0), axis=0, keepdims=True)


def _srest_call(hq, hf, hi, qm, state, cmem, la, lc, oml, on, *, layer):
    nb = hq.shape[0]
    per_b = lambda rows, wd: pl.BlockSpec((1, rows, wd), lambda b: (b, 0, 0))
    per_lb = lambda rows, wd: pl.BlockSpec((1, rows, wd), lambda b: (layer * nb + b, 0, 0))
    full = lambda a: pl.BlockSpec(a.shape, lambda b: (0, 0))
    return pl.pallas_call(
        _srest_kernel,
        grid=(nb,),
        in_specs=[per_b(1, 256)] * 4 + [per_lb(GROUP_W, HEAD_DIM), per_lb(2 * GROUP_W, N_MEM),
                                        full(la), full(lc), full(oml), full(on)],
        out_specs=[per_b(1, 256), per_b(1, 256), per_b(GROUP_W, HEAD_DIM)],
        out_shape=[jax.ShapeDtypeStruct((nb, 1, 256), F32)] * 2 + [jax.ShapeDtypeStruct((nb, GROUP_W, HEAD_DIM), F32)],
        compiler_params=_cparams(("parallel",)),
    )(hq, hf, hi, qm, state, cmem, la, lc, oml, on)


def _sfinal_kernel(x_ref, onsa_ref, ohg_ref, u_ref, buf_ref, omem_ref, z_ref, pw_ref, ps_ref, wo_ref, y_ref):
    u = u_ref[...]
    lane_g = _lane_head(u.shape)
    run = u
    pooled = jnp.zeros(u.shape, F32)
    nxt = 1
    for gi, wsz in enumerate(POOL_SIZES):
        for sh in range(nxt, wsz):
            run = run + buf_ref[0, POOL_BUF - sh]
        nxt = wsz
        pooled = jnp.where(lane_g == gi, run / float(wsz), pooled)
    o_pool = _hdot(pooled - u, pw_ref[...]) * ps_ref[...]
    z = z_ref[...]
    y = jnp.concatenate([onsa_ref[...], ohg_ref[...], o_pool, omem_ref[...]], axis=1) * (z * _sigmoid(z))
    y_ref[...] = x_ref[...] + _hdot(y, wo_ref[...])


def _sfinal_call(x, onsa, ohg, u, buf, omem, z, pw, ps, wo, *, layer):
    full = lambda a: pl.BlockSpec(a.shape, lambda i: (0, 0))
    buf_spec = pl.BlockSpec((1,) + buf.shape[1:], lambda i: (layer, 0, 0, 0))
    return pl.pallas_call(
        _sfinal_kernel,
        grid=(1,),
        in_specs=[full(x), full(onsa), full(ohg), full(u), buf_spec, full(omem), full(z), full(pw), full(ps), full(wo)],
        out_specs=full(x),
        out_shape=jax.ShapeDtypeStruct(x.shape, F32),
        compiler_params=_cparams(("arbitrary",)),
    )(x, onsa, ohg, u, buf, omem, z, pw, ps, wo)


def _block_diag_ones(w):
    i = np.arange(w)
    return jnp.asarray((i[:, None] // HEAD_DIM) == (i[None, :] // HEAD_DIM), BF16)


def _rope_tables(pos):
    half = HEAD_DIM // 2
    inv = ROPE_THETA ** (-jnp.arange(half, dtype=jnp.float32) / half)
    ang = pos.astype(jnp.float32)[:, None] * inv[None, :]
    cos = jnp.cos(ang)
    sin = jnp.sin(ang)
    cos64 = jnp.concatenate([cos, cos], axis=1)
    sin64 = jnp.concatenate([-sin, sin], axis=1)
    one = jnp.ones_like(cos64)
    zero = jnp.zeros_like(sin64)
    cq = jnp.tile(cos64, (1, N_HEADS))
    sq = jnp.tile(sin64, (1, N_HEADS))
    ckv = jnp.concatenate([cos64, one] * 3, axis=1)
    skv = jnp.concatenate([sin64, zero] * 3, axis=1)
    return cq, sq, ckv, skv


def _layer_params(l, norm_g, w_in, w_out, nsa_qn, nsa_kn, cmp_pe, cmp_w1, cmp_w2, lbs, hg_on, pool_w, pool_scale,
                  mem_norm, w_mem_kv, mem_qn, mem_kn):
    w = w_in[l]
    w_pad = jnp.concatenate([w[:, :652], jnp.zeros((D_MODEL, 116), F32), w[:, 652:]], axis=1)
    ones64 = jnp.ones((HEAD_DIM,), F32)
    kn = jnp.concatenate([nsa_kn[l, 0], ones64, nsa_kn[l, 1], ones64, nsa_kn[l, 2], ones64])[None, :]
    w1 = cmp_w1[l].reshape(2, L_CMP, HEAD_DIM, CMP_HID)
    w1e = jnp.zeros((L_CMP, 4, HEAD_DIM, 2 * CMP_HID), F32)
    w1e = w1e.at[:, 0, :, :CMP_HID].set(w1[0]).at[:, 1, :, CMP_HID:].set(w1[1])
    w2e = jnp.zeros((2 * CMP_HID, 128), F32)
    w2e = w2e.at[:CMP_HID, :HEAD_DIM].set(cmp_w2[l, 0]).at[CMP_HID:, HEAD_DIM:].set(cmp_w2[l, 1])
    pe = jnp.zeros((L_CMP, 4, HEAD_DIM), F32).at[:, 0].set(cmp_pe[l, 0]).at[:, 1].set(cmp_pe[l, 1])
    wr = jnp.zeros((L_CMP, 2, HEAD_DIM, 2 * CMP_HID), F32)
    wr = wr.at[:, 0, :, :CMP_HID].set(w1[0]).at[:, 1, :, CMP_HID:].set(w1[1])
    pw = jnp.zeros((GROUP_W, GROUP_W), F32)
    for g in range(len(POOL_SIZES)):
        pw = pw.at[g * 64:(g + 1) * 64, g * 64:(g + 1) * 64].set(pool_w[l, g])
    lb = lbs[l][None, :]
    return dict(
        g=norm_g[l][None, :], w32=w_pad, w16=w_pad.astype(BF16), wo32=w_out[l], wo16=w_out[l].astype(BF16),
        qn=jnp.tile(nsa_qn[l], N_HEADS)[None, :], kn=kn, mqn=jnp.tile(mem_qn[l], N_HEADS)[None, :],
        pe=pe.reshape(1, L_CMP * 256), w1e=w1e.reshape(L_CMP * 256, 2 * CMP_HID).astype(BF16), w2e=w2e.astype(BF16),
        pex=jnp.concatenate([cmp_pe[l, 0], cmp_pe[l, 1]], axis=1), wr=wr.reshape(L_CMP, 128, 2 * CMP_HID).astype(BF16),
        la=jnp.log(lb), lc=jnp.log1p(-lb), oml=1.0 - lb, on=hg_on[l][None, :],
        pw32=pw, pw16=pw.astype(BF16), ps=pool_scale[l][None, :],
        mg=mem_norm[l][None, :], wm16=w_mem_kv[l].astype(BF16), mkn=jnp.tile(mem_kn[l], N_HEADS)[None, :],
    )


def _prompt_layer(x, mem, p, tabs, consts):
    b, t, _ = x.shape
    bd256, bd384, d16, mk = consts
    outs = _proj_call(x.reshape(b * t, D_MODEL), p['g'], p['w16'], *tabs, p['qn'], p['kn'], p['mqn'], bd256, bd384,
                      tm=512, t_tiles=t // 512, precise=False)
    q, nsa, win, gates, hq, hf, hi, u, qm, z = [o.reshape(b, t, -1) for o in outs]
    n_blk = t // L_CMP
    cc = _cmp_call(nsa.reshape(b * n_blk, L_CMP * 256), p['pe'], p['w1e'], p['w2e'],
                   tr=min(256, b * n_blk), n_rows=b * n_blk, row_off=0)
    cc = jnp.pad(cc.reshape(b, n_blk, 128), ((0, 0), (0, 128 - n_blk), (0, 0)))
    o_nsa = _nsa_call(q, nsa, win, gates, cc, tq=NSA_TQ)
    o_hg, st = _hgrn_call(hq, hf, hi, p['la'], p['lc'], p['oml'], p['on'], d16, mk, bd256)
    mkv = _memkv_call(mem, p['mg'], p['wm16'], p['mkn'], bd256)
    y = _final_call(x, o_nsa, o_hg, u, qm, z, mkv, p['pw16'], p['ps'], p['wo16'], tm=256)
    st5 = st.reshape(b, N_HEADS, HEAD_DIM, N_HEADS, HEAD_DIM)
    s_new = jnp.stack([st5[:, h, :, h, :] for h in range(N_HEADS)], axis=1)
    s_new = jnp.swapaxes(s_new, 2, 3)
    wb = min(WINDOW, t)
    return (y, nsa.reshape(b, t, 4, 1, HEAD_DIM), win[:, t - wb:].reshape(b, wb, 2, 1, HEAD_DIM), s_new,
            u[:, t - POOL_BUF:], mkv.reshape(b, N_MEM, 2, N_HEADS, HEAD_DIM))


def _sample_layer(l, x, p, tabs, consts, cache_pages, cache_pairs, win_cache, state, pool_buf, cmem, page_table,
                  n_phys):
    nb = x.shape[0]
    bd256, bd384, _, _ = consts
    outs = _proj_call(x, p['g'], p['w32'], *tabs, p['qn'], p['kn'], p['mqn'], bd256, bd384,
                      tm=nb, t_tiles=1, precise=True)
    q, nsa, win, gates, hq, hf, hi, u, qm, z = outs
    cc = _cmp_pages_call(cache_pages, p['pex'], p['wr'], p['w2e'], pp=64, n_pages=n_phys, page_off=l * n_phys)
    idx, o_c = _scmp_call(page_table, q, cc.reshape(n_phys, 512))
    r3 = lambda a: a.reshape(nb, 1, -1)
    o_nsa = _ssel_call(page_table, idx, cache_pairs, r3(q), r3(nsa), win_cache, r3(win), r3(gates),
                       o_c.reshape(nb, 8, HEAD_DIM), layer=l, n_phys=n_phys)
    o_hg, o_mem, s_new = _srest_call(r3(hq), r3(hf), r3(hi), r3(qm), state, cmem,
                                     p['la'], p['lc'], p['oml'], p['on'], layer=l)
    y = _sfinal_call(x, o_nsa.reshape(nb, 256), o_hg.reshape(nb, 256), u, pool_buf, o_mem.reshape(nb, 256), z,
                     p['pw32'], p['ps'], p['wo32'], layer=l)
    return y, nsa, win, s_new.reshape(nb, N_HEADS, HEAD_DIM, HEAD_DIM), u


def kernel(x_prompt, x_sample, mem_prompt, cache_nsa, cache_nsa_win, state_hgrn, state_pool, cache_mem,
           page_table, norm_g, w_in, w_out, nsa_qn, nsa_kn, cmp_pe, cmp_w1, cmp_w2, hg_lb, hg_on,
           pool_w, pool_scale, mem_norm, w_mem_kv, mem_qn, mem_kn):
    depth = w_in.shape[0]
    b, t, _ = x_prompt.shape
    nb = x_sample.shape[0]
    past_len = page_table.shape[1] * PAGE_SIZE
    n_phys = cache_nsa.shape[1]
    wb = cache_nsa_win.shape[2]

    lbs = jnp.cumsum(jax.nn.softmax(hg_lb.astype(jnp.float32), axis=0), axis=0)
    lbs = lbs - lbs[0:1]
    d_np, mk_np = _hgrn_consts()
    consts = (_block_diag_ones(256), _block_diag_ones(384), jnp.asarray(d_np, BF16), jnp.asarray(mk_np, F32))
    tabs_p = _rope_tables(jnp.arange(t))
    tabs_s = tuple(jnp.broadcast_to(a, (nb, a.shape[1])) for a in _rope_tables(past_len + jnp.arange(1)))

    cache_t = jnp.transpose(cache_nsa, (0, 1, 3, 4, 5, 2))
    cache_pages = cache_t.reshape(depth * n_phys, 4 * HEAD_DIM, PAGE_SIZE)
    cache_pairs = cache_t.reshape(depth * n_phys * 2, 2 * HEAD_DIM, PAGE_SIZE)
    win_cache = jnp.transpose(cache_nsa_win, (0, 1, 3, 4, 5, 2)).reshape(depth * nb, 2 * HEAD_DIM, wb)
    state = state_hgrn.reshape(depth * nb, GROUP_W, HEAD_DIM)
    pool_buf = jnp.transpose(state_pool, (0, 2, 1, 3))
    cmem = jnp.transpose(cache_mem, (0, 1, 3, 4, 5, 2)).reshape(depth * nb, 2 * GROUP_W, N_MEM)

    xp, xs = x_prompt, x_sample.reshape(nb, D_MODEL)
    acc = [[] for _ in range(9)]
    for l in range(depth):
        p = _layer_params(l, norm_g, w_in, w_out, nsa_qn, nsa_kn, cmp_pe, cmp_w1, cmp_w2, lbs, hg_on, pool_w,
                          pool_scale, mem_norm, w_mem_kv, mem_qn, mem_kn)
        xp, a, bw, c, d, e = _prompt_layer(xp, mem_prompt, p, tabs_p, consts)
        xs, sa, sw, sc, sd = _sample_layer(l, xs, p, tabs_s, consts, cache_pages, cache_pairs, win_cache, state,
                                           pool_buf, cmem, page_table, n_phys)
        new_win_s = jnp.concatenate([cache_nsa_win[l][:, 1:], sw.reshape(nb, 1, 2, 1, HEAD_DIM)], axis=1)[:, -wb:]
        new_pool_s = jnp.concatenate([state_pool[l][:, 1:], sd.reshape(nb, 1, GROUP_W)], axis=1)
        for lst, val in zip(acc, (a, bw, c, d, e, sa.reshape(nb, 1, 4, 1, HEAD_DIM), new_win_s, sc, new_pool_s)):
            lst.append(val)
    return (xp, xs.reshape(nb, 1, D_MODEL)) + tuple(jnp.stack(v) for v in acc)
```

```python
import functools

import numpy as np
import jax
import jax.numpy as jnp
from jax import lax
from jax.experimental import pallas as pl
from jax.experimental.pallas import tpu as pltpu

D_MODEL = 1024
HEAD_DIM = 64
GROUP_W = 256
N_HEADS = 4
L_CMP = 32
L_SLC = 64
N_SEL = 16
WINDOW = 512
CMP_HID = 256
POOL_SIZES = (2, 4, 8, 16)
POOL_BUF = 15
N_MEM = 256
PAGE_SIZE = 128
ROPE_THETA = 10000.0
EPS = 1e-6
QK_SCALE = HEAD_DIM ** -0.5

PROJ_W = 3072
HG_CHUNK = 256
NSA_TQ = 256
NEG = -1e30
VMEM_LIMIT = 56 * 1024 * 1024

F32 = jnp.float32
BF16 = jnp.bfloat16
HI = lax.Precision.HIGHEST


def _cparams(sem):
    return pltpu.CompilerParams(dimension_semantics=sem, vmem_limit_bytes=VMEM_LIMIT)


def _bdot(a, b):
    return jnp.dot(a.astype(BF16), b.astype(BF16), preferred_element_type=F32)


def _bdot_nt(a, b):
    return lax.dot_general(a.astype(BF16), b.astype(BF16), (((1,), (1,)), ((), ())), preferred_element_type=F32)


def _hdot(a, b):
    return jnp.dot(a, b, precision=HI, preferred_element_type=F32)


def _hdot_nt(a, b):
    return lax.dot_general(a, b, (((1,), (1,)), ((), ())), precision=HI, preferred_element_type=F32)


def _dot2(a, b16):
    hi = a.astype(BF16)
    lo = (a - hi.astype(F32)).astype(BF16)
    return jnp.dot(hi, b16, preferred_element_type=F32) + jnp.dot(lo, b16, preferred_element_type=F32)


def _dot2_left(a16, b):
    hi = b.astype(BF16)
    lo = (b - hi.astype(F32)).astype(BF16)
    return jnp.dot(a16, hi, preferred_element_type=F32) + jnp.dot(a16, lo, preferred_element_type=F32)


def _seg_rms(x, bd16, gain):
    ms = _dot2(x * x, bd16) * (1.0 / HEAD_DIM)
    return x * lax.rsqrt(ms + EPS) * gain


def _rope(x, cos, sin_signed):
    w = x.shape[1]
    lane = lax.broadcasted_iota(jnp.int32, x.shape, 1)
    first = (lane & 32) == 0
    rot = jnp.where(first, pltpu.roll(x, w - 32, 1), pltpu.roll(x, 32, 1))
    return x * cos + rot * sin_signed


def _sigmoid(x):
    return jax.nn.sigmoid(x)


def _lane_head(shape):
    return lax.broadcasted_iota(jnp.int32, shape, 1) // HEAD_DIM


def _proj_kernel(x_ref, g_ref, w_ref, cq_ref, sq_ref, ckv_ref, skv_ref, qn_ref, kn_ref, mqn_ref, bd256_ref, bd384_ref,
                 q_out, nsa_out, win_out, gate_out, hq_out, hf_out, hi_out, u_out, qm_out, z_out, *, precise):
    x = x_ref[...]
    ms = jnp.mean(x * x, axis=-1, keepdims=True)
    h = x * lax.rsqrt(ms + EPS) * g_ref[...]
    proj = _hdot(h, w_ref[...]) if precise else _bdot(h, w_ref[...])
    bd256 = bd256_ref[...]
    q = _rope(_seg_rms(proj[:, 0:256], bd256, qn_ref[...]), cq_ref[...], sq_ref[...])
    q_out[...] = q
    kv = proj[:, 256:640]
    kvn = _rope(_seg_rms(kv, bd384_ref[...], kn_ref[...]), ckv_ref[...], skv_ref[...])
    lane = lax.broadcasted_iota(jnp.int32, kv.shape, 1)
    is_key = ((lane // HEAD_DIM) & 1) == 0
    kv = jnp.where(is_key, kvn, kv)
    nsa_out[...] = kv[:, 0:256]
    win_out[...] = kv[:, 256:384]
    gate_out[...] = _sigmoid(proj[:, 640:768])
    hq_out[...] = proj[:, 768:1024]
    hf_out[...] = proj[:, 1024:1280]
    hi_out[...] = proj[:, 1280:1536]
    u_out[...] = proj[:, 1536:1792]
    qm_out[...] = _seg_rms(proj[:, 1792:2048], bd256, mqn_ref[...])
    z_out[...] = proj[:, 2048:3072]


def _proj_call(x, g, w, cq, sq, ckv, skv, qn, kn, mqn, bd256, bd384, *, tm, t_tiles, precise):
    n = x.shape[0]
    row = lambda wd: pl.BlockSpec((tm, wd), lambda i: (i, 0))
    tab = lambda wd: pl.BlockSpec((tm, wd), lambda i: (i % t_tiles, 0))
    full = lambda a: pl.BlockSpec(a.shape, lambda i: (0, 0))
    widths = (256, 256, 128, 128, 256, 256, 256, 256, 256, 1024)
    return pl.pallas_call(
        functools.partial(_proj_kernel, precise=precise),
        grid=(n // tm,),
        in_specs=[row(D_MODEL), full(g), full(w), tab(256), tab(256), tab(384), tab(384),
                  full(qn), full(kn), full(mqn), full(bd256), full(bd384)],
        out_specs=[row(wd) for wd in widths],
        out_shape=[jax.ShapeDtypeStruct((n, wd), F32) for wd in widths],
        compiler_params=_cparams(("parallel",)),
    )(x, g, w, cq, sq, ckv, skv, qn, kn, mqn, bd256, bd384)


def _cmp_kernel(x_ref, pe_ref, w1_ref, w2_ref, o_ref):
    x = x_ref[...] + pe_ref[...]
    h = _bdot(x, w1_ref[...])
    h = h * _sigmoid(h)
    o_ref[...] = _bdot(h, w2_ref[...])


def _cmp_call(x, pe, w1, w2, *, tr, n_rows, row_off):
    off = row_off // tr
    full = lambda a: pl.BlockSpec(a.shape, lambda i: (0, 0))
    return pl.pallas_call(
        _cmp_kernel,
        grid=(n_rows // tr,),
        in_specs=[pl.BlockSpec((tr, x.shape[1]), lambda i: (i + off, 0)), full(pe), full(w1), full(w2)],
        out_specs=pl.BlockSpec((tr, 128), lambda i: (i, 0)),
        out_shape=jax.ShapeDtypeStruct((n_rows, 128), F32),
        compiler_params=_cparams(("parallel",)),
    )(x, pe, w1, w2)


def _cmp_pages_kernel(pg_ref, pe_ref, perm_ref, wr_ref, w2_ref, o_ref, xs_scr):
    npg = pg_ref.shape[0]
    pe2 = pe_ref[...]
    perm = perm_ref[...]

    for i in range(npg // 2):
        m2 = jnp.concatenate([pg_ref[2 * i], pg_ref[2 * i + 1]], axis=1) + pe2
        xp = lax.dot_general(perm, m2.astype(BF16), (((1,), (1,)), ((), ())), preferred_element_type=F32)
        xs_scr[:, 8 * i:8 * i + 8, :] = xp.reshape(L_CMP, 8, 128)
    acc = jnp.zeros((npg * (PAGE_SIZE // L_CMP), 2 * CMP_HID), F32)
    for r2 in range(L_CMP // 2):
        slab = jnp.concatenate([xs_scr[2 * r2], xs_scr[2 * r2 + 1]], axis=1)
        acc = acc + _bdot(slab, wr_ref[r2])
    h = acc * _sigmoid(acc)
    o_ref[...] = _bdot(h, w2_ref[...])


def _cmp_pages_call(pages, pe2, perm, wr2, w2, *, pp, n_pages, page_off):
    off = page_off // pp
    n_out = pp * (PAGE_SIZE // L_CMP)
    full = lambda a: pl.BlockSpec(a.shape, lambda i: (0,) * a.ndim)
    return pl.pallas_call(
        _cmp_pages_kernel,
        grid=(n_pages // pp,),
        in_specs=[pl.BlockSpec((pp, 128, PAGE_SIZE), lambda i: (i + off, 0, 0)),
                  full(pe2), full(perm), full(wr2), full(w2)],
        out_specs=pl.BlockSpec((n_out, 128), lambda i: (i, 0)),
        out_shape=jax.ShapeDtypeStruct((n_pages * (PAGE_SIZE // L_CMP), 128), F32),
        scratch_shapes=[pltpu.VMEM((L_CMP, n_out, 128), F32)],
        compiler_params=_cparams(("parallel",)),
    )(pages, pe2, perm, wr2, w2)


def _pair_perm():
    a = np.zeros((L_CMP, 8, 2, PAGE_SIZE), np.float32)
    for r in range(L_CMP):
        for pq in range(2):
            for nb in range(PAGE_SIZE // L_CMP):
                a[r, 4 * pq + nb, pq, L_CMP * nb + r] = 1.0
    return jnp.asarray(a.reshape(L_CMP * 8, 2 * PAGE_SIZE), BF16)


def _topk_mask(score, k, idx_f, axis=1):
    sel = jnp.zeros(score.shape, jnp.bool_)
    s = score
    big = jnp.float32(1e9)
    picks = []
    for _ in range(k):
        m = jnp.max(s, axis=axis, keepdims=True)
        cand = jnp.where(s == m, idx_f, big)
        pick = jnp.min(cand, axis=axis, keepdims=True)
        hit = idx_f == pick
        sel = sel | hit
        s = jnp.where(hit, -jnp.inf, s)
        picks.append(pick)
    return sel, picks


def _flash_step(carry, k16, vt16, bias, qs_scr):
    m, l, acc = carry
    s = lax.dot_general(k16, qs_scr[...], (((1,), (1,)), ((), ())), preferred_element_type=F32)
    s = s + (jnp.concatenate([bias] * N_HEADS, axis=1) if bias.ndim == 2 else bias)
    m_new = jnp.maximum(m, jnp.max(s, axis=0, keepdims=True))
    alpha = jnp.exp(m - m_new)
    p = jnp.exp(s - m_new)
    l = alpha * l + jnp.sum(p, axis=0, keepdims=True)
    acc = alpha * acc + jnp.dot(vt16, p.astype(BF16), preferred_element_type=F32)
    return m_new, l, acc


def _nsa_kernel(q_ref, nsa_ref, win_ref, gate_ref, cc_ref, o_ref, qs_scr, vt_scr, wt_scr, *, tq):
    i = pl.program_id(1)
    t_len = nsa_ref.shape[1]
    assert WINDOW == 2 * tq and t_len % tq == 0

    @pl.when(i == 0)
    def _():
        for c in range(t_len // tq):
            sl = slice(c * tq, (c + 1) * tq)
            vt_scr[c] = nsa_ref[0, sl, 128:256].T.astype(BF16)
            wt_scr[c] = win_ref[0, sl, :].T.astype(BF16)

    t0 = i * tq
    r4 = N_HEADS * tq
    q = q_ref[0] * QK_SCALE
    for h in range(N_HEADS):
        qs_scr[h * tq:(h + 1) * tq, :] = q[:, h * HEAD_DIM:(h + 1) * HEAD_DIM].astype(BF16)
    t_pos = t0 + lax.broadcasted_iota(jnp.int32, (1, tq), 1)
    t_pos4 = t0 + lax.broadcasted_iota(jnp.int32, (1, r4), 1) % tq

    n_cmp = HEAD_DIM
    cc = cc_ref[0]
    kcc16 = cc[0:n_cmp, 0:HEAD_DIM].astype(BF16)
    vcct16 = cc.T[HEAD_DIM:2 * HEAD_DIM, 0:n_cmp].astype(BF16)
    done = (lax.broadcasted_iota(jnp.int32, (n_cmp, r4), 0) + 1) * L_CMP - 1 <= t_pos4
    s = lax.dot_general(kcc16, qs_scr[...], (((1,), (1,)), ((), ())), preferred_element_type=F32)
    s = jnp.where(done, s, NEG)
    m = jnp.max(s, axis=0, keepdims=True)
    m = jnp.where(m < 0.5 * NEG, 0.0, m)
    p = jnp.where(done, jnp.exp(s - m), 0.0)
    p = p / jnp.maximum(jnp.sum(p, axis=0, keepdims=True), 1e-30)
    o_c = jnp.dot(vcct16, p.astype(BF16), preferred_element_type=F32)
    imp = p[:, 0:tq] + p[:, tq:2 * tq] + p[:, 2 * tq:3 * tq] + p[:, 3 * tq:4 * tq]

    n_idx = lax.broadcasted_iota(jnp.int32, (n_cmp, tq), 0)
    imp2 = imp + pltpu.roll(imp, n_cmp - 1, 0)
    blk_t = t_pos // L_SLC
    j = n_idx // 2
    valid = ((n_idx & 1) == 0) & (j < t_len // L_SLC)
    avail = valid & (j <= blk_t)
    forced = (j == 0) | (j == blk_t) | (j == blk_t - 1)
    score = jnp.where(avail, jnp.where(forced, jnp.inf, imp2), -jnp.inf)
    sel, _ = _topk_mask(score, N_SEL, n_idx.astype(F32), axis=0)
    sel16 = jnp.where(sel & avail, 1.0, 0.0).astype(BF16)

    key_l = lax.broadcasted_iota(jnp.int32, (tq, tq), 0)
    qry_l = lax.broadcasted_iota(jnp.int32, (tq, tq), 1)
    causal = jnp.where(key_l <= qry_l, 0.0, NEG)
    tail = jnp.where(key_l > qry_l, 0.0, NEG)

    def sel_bias(c):
        ek = c * tq + lax.broadcasted_iota(jnp.int32, (tq, n_cmp), 0)
        en = lax.broadcasted_iota(jnp.int32, (tq, n_cmp), 1)
        expand = jnp.where(en == 2 * (ek // L_SLC), 1.0, 0.0).astype(BF16)
        msel = jnp.dot(expand, sel16, preferred_element_type=F32)
        return (msel - 1.0) * (-NEG)

    def chunk(ref, vt, c, lo):
        start = pl.multiple_of(c * tq, tq)
        return ref[0, pl.ds(start, tq), lo:lo + HEAD_DIM].astype(BF16), vt[c, HEAD_DIM:2 * HEAD_DIM, :]

    init = (jnp.full((1, r4), NEG, F32), jnp.zeros((1, r4), F32), jnp.zeros((HEAD_DIM, r4), F32))

    def sel_body(c, carry):
        k16, vt16 = chunk(nsa_ref, vt_scr, c, 128)
        return _flash_step(carry, k16, vt16, sel_bias(c), qs_scr)

    st_s = lax.fori_loop(0, i, sel_body, init)
    k16, vt16 = chunk(nsa_ref, vt_scr, i, 128)
    st_s = _flash_step(st_s, k16, vt16, sel_bias(i) + causal, qs_scr)

    k16, vt16 = chunk(win_ref, wt_scr, i, 0)
    st_w = _flash_step(init, k16, vt16, causal, qs_scr)
    k16, vt16 = chunk(win_ref, wt_scr, jnp.maximum(i - 1, 0), 0)
    st_w = _flash_step(st_w, k16, vt16, jnp.where(i >= 1, 0.0, NEG), qs_scr)
    k16, vt16 = chunk(win_ref, wt_scr, jnp.maximum(i - 2, 0), 0)
    st_w = _flash_step(st_w, k16, vt16, tail + jnp.where(i >= 2, 0.0, NEG), qs_scr)

    gt = gate_ref[0].T
    g_c, g_s, g_w = (jnp.concatenate([gt[3 * h + br:3 * h + br + 1] for h in range(N_HEADS)], axis=1)
                     for br in range(3))
    o_s = st_s[2] / jnp.maximum(st_s[1], 1e-30)
    o_w = st_w[2] / jnp.maximum(st_w[1], 1e-30)
    o = g_c * o_c + g_s * o_s + g_w * o_w
    o_ref[0] = jnp.concatenate([o[:, h * tq:(h + 1) * tq] for h in range(N_HEADS)], axis=0).T


def _nsa_call(q, nsa, win, gates, cc, *, tq):
    b, t, _ = q.shape
    return pl.pallas_call(
        functools.partial(_nsa_kernel, tq=tq),
        grid=(b, t // tq),
        in_specs=[pl.BlockSpec((1, tq, 256), lambda bi, i: (bi, i, 0)),
                  pl.BlockSpec((1, t, 256), lambda bi, i: (bi, 0, 0)),
                  pl.BlockSpec((1, t, 128), lambda bi, i: (bi, 0, 0)),
                  pl.BlockSpec((1, tq, 128), lambda bi, i: (bi, i, 0)),
                  pl.BlockSpec((1, 128, 128), lambda bi, i: (bi, 0, 0))],
        out_specs=pl.BlockSpec((1, tq, 256), lambda bi, i: (bi, i, 0)),
        out_shape=jax.ShapeDtypeStruct((b, t, 256), F32),
        scratch_shapes=[pltpu.VMEM((N_HEADS * tq, HEAD_DIM), BF16), pltpu.VMEM((t // tq, 128, tq), BF16),
                        pltpu.VMEM((t // tq, 128, tq), BF16)],
        compiler_params=_cparams(("parallel", "arbitrary")),
    )(q, nsa, win, gates, cc)


def _hgrn_consts(c=HG_CHUNK):
    t = np.arange(c)[:, None]
    k = np.arange(c)[None, :]
    blocks = [k <= t, k > t]
    masks = []
    m = c // 2
    while m >= 1:
        blk = t // (2 * m)
        upper = (t % (2 * m)) >= m
        ref = blk * 2 * m + m - 1
        blocks.append(upper & (k > ref) & (k <= t))
        blocks.append((~upper) & (k > t) & (k <= ref))
        masks.append(upper & ((k // (2 * m)) == blk) & ((k % (2 * m)) < m))
        m //= 2
    blocks.append(np.zeros((c, c), bool))
    blocks.append(np.zeros((c, c), bool))
    masks.append(t == k)
    d = np.concatenate(blocks, 0).astype(np.float32)
    mk = np.stack(masks).astype(np.float32)
    return d, mk


def _log_decay(fr, la, lc):
    ls = jnp.minimum(fr, 0.0) - jnp.log1p(jnp.exp(-jnp.abs(fr)))
    b = lc + ls
    return jnp.maximum(la, b) + jnp.log1p(jnp.exp(-jnp.abs(la - b)))


def _hgrn_kernel(hq_ref, hf_ref, hi_ref, la_ref, lc_ref, oml_ref, on_ref, d_ref, mk_ref, bd_ref,
                 o_ref, st_ref, st_scr):
    c = HG_CHUNK
    jt = pl.program_id(1)

    @pl.when(jt == 0)
    def _():
        st_scr[...] = jnp.zeros_like(st_scr)

    q = hq_ref[0]
    fr = hf_ref[0]
    v = hi_ref[0]
    logf = _log_decay(fr, la_ref[...], lc_ref[...])
    k = oml_ref[...] * _sigmoid(-fr)
    e_all = _dot2_left(d_ref[...], logf)
    gc = e_all[0:c]
    esuf = e_all[c:2 * c]
    lane_h = _lane_head((c, GROUP_W))
    n_lvl = mk_ref.shape[0]
    a_tot = [jnp.zeros((c, c), F32) for _ in range(N_HEADS)]
    for lv in range(n_lvl):
        eq = e_all[(2 + 2 * lv) * c:(3 + 2 * lv) * c]
        ek = e_all[(3 + 2 * lv) * c:(4 + 2 * lv) * c]
        qt = q * jnp.exp(eq)
        kt = (k * jnp.exp(ek)).astype(BF16)
        qs = jnp.concatenate([jnp.where(lane_h == h, qt, 0.0).astype(BF16) for h in range(N_HEADS)], axis=0)
        a_l = lax.dot_general(qs, kt, (((1,), (1,)), ((), ())), preferred_element_type=F32)
        mk = mk_ref[lv]
        for h in range(N_HEADS):
            a_tot[h] = a_tot[h] + mk * a_l[h * c:(h + 1) * c]
    a_all = jnp.concatenate(a_tot, axis=0)
    r = _bdot(a_all, v)
    o = jnp.zeros((c, GROUP_W), F32)
    for h in range(N_HEADS):
        o = o + jnp.where(lane_h == h, r[h * c:(h + 1) * c], 0.0)
    st = st_scr[...]
    o = o + _bdot_nt(q * jnp.exp(gc), st)
    gl = gc[c - 1:c, :]
    khat = k * jnp.exp(esuf)
    upd = jnp.dot(v.T.astype(BF16), khat.astype(BF16), preferred_element_type=F32)
    row_h = lax.broadcasted_iota(jnp.int32, (GROUP_W, GROUP_W), 0) // HEAD_DIM
    col_h = _lane_head((GROUP_W, GROUP_W))
    st_new = jnp.exp(gl) * st + jnp.where(row_h == col_h, upd, 0.0)
    st_scr[...] = st_new
    st_ref[0] = st_new
    o_ref[0] = _seg_rms(o, bd_ref[...], on_ref[...])


def _hgrn_call(hq, hf, hi, la, lc, oml, on, d16, mk, bd256):
    b, t, _ = hq.shape
    c = HG_CHUNK
    row = pl.BlockSpec((1, c, GROUP_W), lambda bi, j: (bi, j, 0))
    full2 = lambda a: pl.BlockSpec(a.shape, lambda bi, j: (0, 0))
    return pl.pallas_call(
        _hgrn_kernel,
        grid=(b, t // c),
        in_specs=[row, row, row, full2(la), full2(lc), full2(oml), full2(on), full2(d16),
                  pl.BlockSpec(mk.shape, lambda bi, j: (0, 0, 0)), full2(bd256)],
        out_specs=[row, pl.BlockSpec((1, GROUP_W, GROUP_W), lambda bi, j: (bi, 0, 0))],
        out_shape=[jax.ShapeDtypeStruct((b, t, GROUP_W), F32), jax.ShapeDtypeStruct((b, GROUP_W, GROUP_W), F32)],
        scratch_shapes=[pltpu.VMEM((GROUP_W, GROUP_W), F32)],
        compiler_params=_cparams(("parallel", "arbitrary")),
    )(hq, hf, hi, la, lc, oml, on, d16, mk, bd256)


def _memkv_kernel(m_ref, g_ref, w_ref, kn_ref, bd_ref, o_ref):
    x = m_ref[0]
    ms = jnp.mean(x * x, axis=-1, keepdims=True)
    h = x * lax.rsqrt(ms + EPS) * g_ref[...]
    kv = _bdot(h, w_ref[...])
    o_ref[0, :, 0:GROUP_W] = _seg_rms(kv[:, 0:GROUP_W], bd_ref[...], kn_ref[...])
    o_ref[0, :, GROUP_W:2 * GROUP_W] = kv[:, GROUP_W:2 * GROUP_W]


def _memkv_call(mem, g, w, kn, bd256):
    b = mem.shape[0]
    full = lambda a: pl.BlockSpec(a.shape, lambda i: (0, 0))
    return pl.pallas_call(
        _memkv_kernel,
        grid=(b,),
        in_specs=[pl.BlockSpec((1, N_MEM, D_MODEL), lambda i: (i, 0, 0)), full(g), full(w), full(kn), full(bd256)],
        out_specs=pl.BlockSpec((1, N_MEM, 2 * GROUP_W), lambda i: (i, 0, 0)),
        out_shape=jax.ShapeDtypeStruct((b, N_MEM, 2 * GROUP_W), F32),
        compiler_params=_cparams(("parallel",)),
    )(mem, g, w, kn, bd256)


def _final_kernel(x_ref, onsa_ref, ohg_ref, u_ref, halo_ref, qm_ref, z_ref, mkv_ref, pw_ref, ps_ref, wo_ref,
                  y_ref, ext_scr, *, tm):
    jt = pl.program_id(1)
    u = u_ref[0]
    halo = halo_ref[0]
    ext_scr[0:16, :] = jnp.where(jt == 0, 0.0, halo)
    ext_scr[16:16 + tm, :] = u
    pos = jt * tm + lax.broadcasted_iota(jnp.int32, (tm, 1), 0)
    lane_g = _lane_head((tm, GROUP_W))
    run = u
    pooled = jnp.zeros((tm, GROUP_W), F32)
    nxt = 1
    for gi, wsz in enumerate(POOL_SIZES):
        for sh in range(nxt, wsz):
            run = run + ext_scr[16 - sh:16 - sh + tm, :]
        nxt = wsz
        cnt = jnp.minimum(wsz, pos + 1).astype(F32)
        pooled = jnp.where(lane_g == gi, run / cnt, pooled)
    o_pool = _bdot(pooled - u, pw_ref[...]) * ps_ref[...]

    qm = qm_ref[0] * QK_SCALE
    mkv = mkv_ref[0]
    heads = []
    for h in range(N_HEADS):
        sl = slice(h * HEAD_DIM, (h + 1) * HEAD_DIM)
        s = _bdot_nt(qm[:, sl], mkv[:, sl])
        s = s - jnp.max(s, axis=1, keepdims=True)
        p = jnp.exp(s)
        p = p / jnp.sum(p, axis=1, keepdims=True)
        heads.append(_bdot(p, mkv[:, GROUP_W + h * HEAD_DIM:GROUP_W + (h + 1) * HEAD_DIM]))
    o_mem = jnp.concatenate(heads, axis=1)

    z = z_ref[0]
    y = jnp.concatenate([onsa_ref[0], ohg_ref[0], o_pool, o_mem], axis=1) * (z * _sigmoid(z))
    y_ref[0] = x_ref[0] + _bdot(y, wo_ref[...])


def _final_call(x, onsa, ohg, u, qm, z, mkv, pw, ps, wo, *, tm):
    b, t, _ = x.shape
    row = lambda wd: pl.BlockSpec((1, tm, wd), lambda bi, j: (bi, j, 0))
    full2 = lambda a: pl.BlockSpec(a.shape, lambda bi, j: (0, 0))
    hb = tm // 16
    return pl.pallas_call(
        functools.partial(_final_kernel, tm=tm),
        grid=(b, t // tm),
        in_specs=[row(D_MODEL), row(GROUP_W), row(GROUP_W), row(GROUP_W),
                  pl.BlockSpec((1, 16, GROUP_W), lambda bi, j: (bi, jnp.maximum(j * hb - 1, 0), 0)),
                  row(GROUP_W), row(D_MODEL),
                  pl.BlockSpec((1, N_MEM, 2 * GROUP_W), lambda bi, j: (bi, 0, 0)),
                  full2(pw), full2(ps), full2(wo)],
        out_specs=row(D_MODEL),
        out_shape=jax.ShapeDtypeStruct((b, t, D_MODEL), F32),
        scratch_shapes=[pltpu.VMEM((tm + 16, GROUP_W), F32)],
        compiler_params=_cparams(("parallel", "arbitrary")),
    )(x, onsa, ohg, u, u, qm, z, mkv, pw, ps, wo)


def _head_rows(row):
    r8 = lax.broadcasted_iota(jnp.int32, (8, GROUP_W), 0)
    qm = jnp.where(r8 == _lane_head((8, GROUP_W)), jnp.broadcast_to(row, (8, GROUP_W)), 0.0)
    return qm[:, 0:64] + qm[:, 64:128] + qm[:, 128:192] + qm[:, 192:256]


def _scmp_kernel(pt_ref, q_ref, cc_hbm, idx_ref, oc_ref, g_scr, a_scr, b_scr, sem, *, nb, npg):
    def copy(n):
        b = n // npg
        pg = pt_ref[b, n - b * npg]
        return pltpu.make_async_copy(cc_hbm.at[pl.ds(pg, 1)], g_scr.at[pl.ds(n, 1)], sem)

    def start(n, c):
        copy(n).start()
        return c

    def wait(n, c):
        copy(n).wait()
        return c

    lax.fori_loop(0, nb * npg, start, 0)
    lax.fori_loop(0, nb * npg, wait, 0)

    def per_batch(b, c):
        gb = g_scr[pl.ds(pl.multiple_of(b * npg, npg), npg), :]
        qh = _head_rows(q_ref[pl.ds(b, 1), :]) * QK_SCALE
        ss = [_hdot_nt(qh, gb[:, r * 128:r * 128 + HEAD_DIM]) for r in range(4)]
        m = ss[0].max(axis=1, keepdims=True)
        for r in range(1, 4):
            m = jnp.maximum(m, ss[r].max(axis=1, keepdims=True))
        ps = [jnp.exp(s - m) for s in ss]
        l = ps[0].sum(axis=1, keepdims=True)
        for r in range(1, 4):
            l = l + ps[r].sum(axis=1, keepdims=True)
        inv = 1.0 / jnp.maximum(l, 1e-30)
        ps = [p * inv for p in ps]
        oc = _hdot(ps[0], gb[:, HEAD_DIM:128])
        for r in range(1, 4):
            oc = oc + _hdot(ps[r], gb[:, r * 128 + HEAD_DIM:(r + 1) * 128])
        oc_ref[pl.ds(pl.multiple_of(b * 8, 8), 8), :] = oc
        hmask = lax.broadcasted_iota(jnp.int32, ss[0].shape, 0) < N_HEADS
        imps = [jnp.sum(jnp.where(hmask, p, 0.0), axis=0, keepdims=True) for p in ps]
        a_scr[pl.ds(b, 1), :] = imps[0] + imps[1]
        b_scr[pl.ds(b, 1), :] = imps[2] + imps[3]
        return c

    lax.fori_loop(0, nb, per_batch, 0)

    score = jnp.concatenate([a_scr[...], b_scr[...]], axis=1)
    lane = lax.broadcasted_iota(jnp.int32, score.shape, 1)
    jidx = jnp.where(lane < npg, 2 * lane, 2 * (lane - npg) + 1)
    forced = (jidx == 0) | (jidx == 2 * npg - 1)
    score = jnp.where(forced, jnp.inf, score)
    _, picks = _topk_mask(score, N_SEL - 1, jidx.astype(F32))
    out_lane = lax.broadcasted_iota(jnp.int32, (nb, 128), 1)
    out = jnp.zeros((nb, 128), jnp.int32)
    for r, pk in enumerate(picks):
        out = jnp.where(out_lane == r, pk.astype(jnp.int32), out)
    idx_ref[...] = out


def _scmp_call(page_table, q, cc_rows):
    nb, npg = page_table.shape
    return pl.pallas_call(
        functools.partial(_scmp_kernel, nb=nb, npg=npg),
        grid_spec=pltpu.PrefetchScalarGridSpec(
            num_scalar_prefetch=1,
            grid=(1,),
            in_specs=[pl.BlockSpec(q.shape, lambda i, pt: (0, 0)), pl.BlockSpec(memory_space=pl.ANY)],
            out_specs=[pl.BlockSpec((nb, 128), lambda i, pt: (0, 0)), pl.BlockSpec((nb * 8, HEAD_DIM), lambda i, pt: (0, 0))],
            scratch_shapes=[pltpu.VMEM((nb * npg, 512), F32), pltpu.VMEM((nb, 128), F32), pltpu.VMEM((nb, 128), F32),
                            pltpu.SemaphoreType.DMA(())],
        ),
        out_shape=[jax.ShapeDtypeStruct((nb, 128), jnp.int32), jax.ShapeDtypeStruct((nb * 8, HEAD_DIM), F32)],
        compiler_params=_cparams(("arbitrary",)),
    )(page_table, q, cc_rows)


def _ssel_kernel(pt_ref, idx_ref, blk_ref, q_ref, new_ref, wc_ref, wnew_ref, gate_ref, oc_ref, o_ref,
                 m_scr, l_scr, a_scr, *, n_gather):
    r = pl.program_id(1)

    @pl.when(r == 0)
    def _():
        m_scr[...] = jnp.full_like(m_scr, NEG)
        l_scr[...] = jnp.zeros_like(l_scr)
        a_scr[...] = jnp.zeros_like(a_scr)

    qh = _head_rows(q_ref[0]) * QK_SCALE
    blk = blk_ref[0]
    half = idx_ref[pl.program_id(0), r] % 2
    s = _hdot(qh, blk[0:HEAD_DIM])
    mine = lax.broadcasted_iota(jnp.int32, s.shape, 1) // L_SLC == half
    s = jnp.where(mine, s, NEG)
    m_old = m_scr[...]
    m_new = jnp.maximum(m_old, s.max(axis=1, keepdims=True))
    alpha = jnp.exp(m_old - m_new)
    p = jnp.where(mine, jnp.exp(s - m_new), 0.0)
    l_scr[...] = alpha * l_scr[...] + p.sum(axis=1, keepdims=True)
    a_scr[...] = alpha * a_scr[...] + _hdot_nt(p, blk[HEAD_DIM:2 * HEAD_DIM])
    m_scr[...] = m_new

    @pl.when(r == n_gather - 1)
    def _():
        new = new_ref[0]
        s_n = jnp.sum(qh * new[:, 128:192], axis=1, keepdims=True)
        m0 = m_scr[...]
        m1 = jnp.maximum(m0, s_n)
        al = jnp.exp(m0 - m1)
        p_n = jnp.exp(s_n - m1)
        l1 = al * l_scr[...] + p_n
        o_s = (al * a_scr[...] + p_n * new[:, 192:256]) / jnp.maximum(l1, 1e-30)
        wc = wc_ref[0]
        wnew = wnew_ref[0]
        s_w = _hdot(qh, wc[0:HEAD_DIM])
        keep = lax.broadcasted_iota(jnp.int32, s_w.shape, 1) >= wc.shape[1] + 1 - WINDOW
        s_w = jnp.where(keep, s_w, NEG)
        s_wn = jnp.sum(qh * wnew[:, 0:HEAD_DIM], axis=1, keepdims=True)
        m_w = jnp.maximum(s_w.max(axis=1, keepdims=True), s_wn)
        p_w = jnp.where(keep, jnp.exp(s_w - m_w), 0.0)
        p_wn = jnp.exp(s_wn - m_w)
        l_w = p_w.sum(axis=1, keepdims=True) + p_wn
        o_w = (_hdot_nt(p_w, wc[HEAD_DIM:128]) + p_wn * wnew[:, HEAD_DIM:128]) / jnp.maximum(l_w, 1e-30)
        o_c = oc_ref[0]
        g = gate_ref[0]
        outs = []
        for h in range(N_HEADS):
            outs.append(g[:, 3 * h:3 * h + 1] * o_c[h:h + 1] + g[:, 3 * h + 1:3 * h + 2] * o_s[h:h + 1]
                        + g[:, 3 * h + 2:3 * h + 3] * o_w[h:h + 1])
        o_ref[0] = jnp.concatenate(outs, axis=1)


def _ssel_call(page_table, idx, cache_pairs, q, new_rows, win_cache, win_new, gates, o_c, *, layer, n_phys):
    nb = q.shape[0]
    n_gather = N_SEL - 1
    wb = win_cache.shape[2]
    pg_off = layer * n_phys

    def blk_map(b, r, pt, ix):
        return ((pg_off + pt[b, ix[b, r] // 2]) * 2 + 1, 0, 0)

    per_b = lambda rows, wd: pl.BlockSpec((1, rows, wd), lambda b, r, pt, ix: (b, 0, 0))
    return pl.pallas_call(
        functools.partial(_ssel_kernel, n_gather=n_gather),
        grid_spec=pltpu.PrefetchScalarGridSpec(
            num_scalar_prefetch=2,
            grid=(nb, n_gather),
            in_specs=[pl.BlockSpec((1, 128, PAGE_SIZE), blk_map), per_b(1, 256), per_b(1, 256),
                      pl.BlockSpec((1, 128, wb), lambda b, r, pt, ix: (layer * nb + b, 0, 0)),
                      per_b(1, 128), per_b(1, 128), per_b(8, HEAD_DIM)],
            out_specs=per_b(1, 256),
            scratch_shapes=[pltpu.VMEM((8, 1), F32), pltpu.VMEM((8, 1), F32), pltpu.VMEM((8, HEAD_DIM), F32)],
        ),
        out_shape=jax.ShapeDtypeStruct((nb, 1, 256), F32),
        compiler_params=_cparams(("arbitrary", "arbitrary")),
    )(page_table, idx, cache_pairs, q, new_rows, win_cache, win_new, gates, o_c)


def _to_col(row, eye):
    return jnp.sum(jnp.where(eye, jnp.broadcast_to(row, eye.shape), 0.0), axis=1, keepdims=True)


def _srest_kernel(hq_ref, hf_ref, hi_ref, qm_ref, st_ref, cm_ref, la_ref, lc_ref, oml_ref, on_ref,
                  ohg_ref, omem_ref, stn_ref):
    q = hq_ref[0]
    fr = hf_ref[0]
    v = hi_ref[0]
    logf = _log_decay(fr, la_ref[...], lc_ref[...])
    f = jnp.exp(logf)
    k = oml_ref[...] * _sigmoid(-fr)
    eye = (lax.broadcasted_iota(jnp.int32, (GROUP_W, GROUP_W), 0)
           == lax.broadcasted_iota(jnp.int32, (GROUP_W, GROUP_W), 1))
    qc, fc, kc = _to_col(q, eye), _to_col(f, eye), _to_col(k, eye)
    vt = jnp.concatenate([jnp.broadcast_to(v[:, h * HEAD_DIM:(h + 1) * HEAD_DIM], (HEAD_DIM, HEAD_DIM))
                          for h in range(N_HEADS)], axis=0)
    s_new = fc * st_ref[0] + kc * vt
    stn_ref[0] = s_new
    x = qc * s_new
    on = on_ref[...]
    outs = []
    for h in range(N_HEADS):
        oh = jnp.sum(x[h * HEAD_DIM:(h + 1) * HEAD_DIM], axis=0, keepdims=True)
        ms = jnp.mean(oh * oh, axis=1, keepdims=True)
        outs.append(oh * lax.rsqrt(ms + EPS) * on[:, h * HEAD_DIM:(h + 1) * HEAD_DIM])
    ohg_ref[0] = jnp.concatenate(outs, axis=1)

    cm = cm_ref[0]
    qrow = qm_ref[0] * QK_SCALE
    r8 = lax.broadcasted_iota(jnp.int32, (8, GROUP_W), 0)
    lh8 = _lane_head((8, GROUP_W))
    qbd = jnp.where(r8 == lh8, jnp.broadcast_to(qrow, (8, GROUP_W)), 0.0)
    s = _hdot(qbd, cm[0:GROUP_W])
    s = s - s.max(axis=1, keepdims=True)
    p = jnp.exp(s)
    p = p / p.sum(axis=1, keepdims=True)
    o8 = _hdot_nt(p, cm[GROUP_W:2 * GROUP_W])
    omem_ref[0] = jnp.sum(jnp.where(r8 == lh8, o8, 0.0), axis=0, keepdims=True)


def _srest_call(hq, hf, hi, qm, state, cmem, la, lc, oml, on, *, layer):
    nb = hq.shape[0]
    per_b = lambda rows, wd: pl.BlockSpec((1, rows, wd), lambda b: (b, 0, 0))
    per_lb = lambda rows, wd: pl.BlockSpec((1, rows, wd), lambda b: (layer * nb + b, 0, 0))
    full = lambda a: pl.BlockSpec(a.shape, lambda b: (0, 0))
    return pl.pallas_call(
        _srest_kernel,
        grid=(nb,),
        in_specs=[per_b(1, 256)] * 4 + [per_lb(GROUP_W, HEAD_DIM), per_lb(2 * GROUP_W, N_MEM),
                                        full(la), full(lc), full(oml), full(on)],
        out_specs=[per_b(1, 256), per_b(1, 256), per_b(GROUP_W, HEAD_DIM)],
        out_shape=[jax.ShapeDtypeStruct((nb, 1, 256), F32)] * 2 + [jax.ShapeDtypeStruct((nb, GROUP_W, HEAD_DIM), F32)],
        compiler_params=_cparams(("parallel",)),
    )(hq, hf, hi, qm, state, cmem, la, lc, oml, on)


def _sfinal_kernel(x_ref, onsa_ref, ohg_ref, u_ref, buf_ref, omem_ref, z_ref, pw_ref, ps_ref, wo_ref, y_ref):
    u = u_ref[...]
    lane_g = _lane_head(u.shape)
    run = u
    pooled = jnp.zeros(u.shape, F32)
    nxt = 1
    for gi, wsz in enumerate(POOL_SIZES):
        for sh in range(nxt, wsz):
            run = run + buf_ref[0, POOL_BUF - sh]
        nxt = wsz
        pooled = jnp.where(lane_g == gi, run / float(wsz), pooled)
    o_pool = _hdot(pooled - u, pw_ref[...]) * ps_ref[...]
    z = z_ref[...]
    y = jnp.concatenate([onsa_ref[...], ohg_ref[...], o_pool, omem_ref[...]], axis=1) * (z * _sigmoid(z))
    y_ref[...] = x_ref[...] + _hdot(y, wo_ref[...])


def _sfinal_call(x, onsa, ohg, u, buf, omem, z, pw, ps, wo, *, layer):
    full = lambda a: pl.BlockSpec(a.shape, lambda i: (0, 0))
    buf_spec = pl.BlockSpec((1,) + buf.shape[1:], lambda i: (layer, 0, 0, 0))
    return pl.pallas_call(
        _sfinal_kernel,
        grid=(1,),
        in_specs=[full(x), full(onsa), full(ohg), full(u), buf_spec, full(omem), full(z), full(pw), full(ps), full(wo)],
        out_specs=full(x),
        out_shape=jax.ShapeDtypeStruct(x.shape, F32),
        compiler_params=_cparams(("arbitrary",)),
    )(x, onsa, ohg, u, buf, omem, z, pw, ps, wo)


def _block_diag_ones(w):
    i = np.arange(w)
    return jnp.asarray((i[:, None] // HEAD_DIM) == (i[None, :] // HEAD_DIM), BF16)


def _rope_tables(pos):
    half = HEAD_DIM // 2
    inv = ROPE_THETA ** (-jnp.arange(half, dtype=jnp.float32) / half)
    ang = pos.astype(jnp.float32)[:, None] * inv[None, :]
    cos = jnp.cos(ang)
    sin = jnp.sin(ang)
    cos64 = jnp.concatenate([cos, cos], axis=1)
    sin64 = jnp.concatenate([-sin, sin], axis=1)
    one = jnp.ones_like(cos64)
    zero = jnp.zeros_like(sin64)
    cq = jnp.tile(cos64, (1, N_HEADS))
    sq = jnp.tile(sin64, (1, N_HEADS))
    ckv = jnp.concatenate([cos64, one] * 3, axis=1)
    skv = jnp.concatenate([sin64, zero] * 3, axis=1)
    return cq, sq, ckv, skv


def _layer_params(l, norm_g, w_in, w_out, nsa_qn, nsa_kn, cmp_pe, cmp_w1, cmp_w2, lbs, hg_on, pool_w, pool_scale,
                  mem_norm, w_mem_kv, mem_qn, mem_kn):
    w = w_in[l]
    w_pad = jnp.concatenate([w[:, :652], jnp.zeros((D_MODEL, 116), F32), w[:, 652:]], axis=1)
    ones64 = jnp.ones((HEAD_DIM,), F32)
    kn = jnp.concatenate([nsa_kn[l, 0], ones64, nsa_kn[l, 1], ones64, nsa_kn[l, 2], ones64])[None, :]
    w1 = cmp_w1[l].reshape(2, L_CMP, HEAD_DIM, CMP_HID)
    w1e = jnp.zeros((L_CMP, 4, HEAD_DIM, 2 * CMP_HID), F32)
    w1e = w1e.at[:, 0, :, :CMP_HID].set(w1[0]).at[:, 1, :, CMP_HID:].set(w1[1])
    w2e = jnp.zeros((2 * CMP_HID, 128), F32)
    w2e = w2e.at[:CMP_HID, :HEAD_DIM].set(cmp_w2[l, 0]).at[CMP_HID:, HEAD_DIM:].set(cmp_w2[l, 1])
    pe = jnp.zeros((L_CMP, 4, HEAD_DIM), F32).at[:, 0].set(cmp_pe[l, 0]).at[:, 1].set(cmp_pe[l, 1])
    wr = jnp.zeros((L_CMP, 2, HEAD_DIM, 2 * CMP_HID), F32)
    wr = wr.at[:, 0, :, :CMP_HID].set(w1[0]).at[:, 1, :, CMP_HID:].set(w1[1])
    pw = jnp.zeros((GROUP_W, GROUP_W), F32)
    for g in range(len(POOL_SIZES)):
        pw = pw.at[g * 64:(g + 1) * 64, g * 64:(g + 1) * 64].set(pool_w[l, g])
    lb = lbs[l][None, :]
    return dict(
        g=norm_g[l][None, :], w32=w_pad, w16=w_pad.astype(BF16), wo32=w_out[l], wo16=w_out[l].astype(BF16),
        qn=jnp.tile(nsa_qn[l], N_HEADS)[None, :], kn=kn, mqn=jnp.tile(mem_qn[l], N_HEADS)[None, :],
        pe=pe.reshape(1, L_CMP * 256), w1e=w1e.reshape(L_CMP * 256, 2 * CMP_HID).astype(BF16), w2e=w2e.astype(BF16),
        pe2=jnp.tile(jnp.concatenate([cmp_pe[l, 0].T, cmp_pe[l, 1].T], axis=0), (1, 2 * PAGE_SIZE // L_CMP)),
        wr2=wr.reshape(L_CMP // 2, 256, 2 * CMP_HID).astype(BF16),
        la=jnp.log(lb), lc=jnp.log1p(-lb), oml=1.0 - lb, on=hg_on[l][None, :],
        pw32=pw, pw16=pw.astype(BF16), ps=pool_scale[l][None, :],
        mg=mem_norm[l][None, :], wm16=w_mem_kv[l].astype(BF16), mkn=jnp.tile(mem_kn[l], N_HEADS)[None, :],
    )


def _prompt_layer(x, mem, p, tabs, consts):
    b, t, _ = x.shape
    bd256, bd384, d16, mk, _ = consts
    outs = _proj_call(x.reshape(b * t, D_MODEL), p['g'], p['w16'], *tabs, p['qn'], p['kn'], p['mqn'], bd256, bd384,
                      tm=512, t_tiles=t // 512, precise=False)
    q, nsa, win, gates, hq, hf, hi, u, qm, z = [o.reshape(b, t, -1) for o in outs]
    n_blk = t // L_CMP
    cc = _cmp_call(nsa.reshape(b * n_blk, L_CMP * 256), p['pe'], p['w1e'], p['w2e'],
                   tr=min(256, b * n_blk), n_rows=b * n_blk, row_off=0)
    cc = jnp.pad(cc.reshape(b, n_blk, 128), ((0, 0), (0, 128 - n_blk), (0, 0)))
    o_nsa = _nsa_call(q, nsa, win, gates, cc, tq=NSA_TQ)
    o_hg, st = _hgrn_call(hq, hf, hi, p['la'], p['lc'], p['oml'], p['on'], d16, mk, bd256)
    mkv = _memkv_call(mem, p['mg'], p['wm16'], p['mkn'], bd256)
    y = _final_call(x, o_nsa, o_hg, u, qm, z, mkv, p['pw16'], p['ps'], p['wo16'], tm=256)
    st5 = st.reshape(b, N_HEADS, HEAD_DIM, N_HEADS, HEAD_DIM)
    s_new = jnp.stack([st5[:, h, :, h, :] for h in range(N_HEADS)], axis=1)
    s_new = jnp.swapaxes(s_new, 2, 3)
    wb = min(WINDOW, t)
    return (y, nsa.reshape(b, t, 4, 1, HEAD_DIM), win[:, t - wb:].reshape(b, wb, 2, 1, HEAD_DIM), s_new,
            u[:, t - POOL_BUF:], mkv.reshape(b, N_MEM, 2, N_HEADS, HEAD_DIM))


def _sample_layer(l, x, p, tabs, consts, cache_pages, cache_pairs, win_cache, state, pool_buf, cmem, page_table,
                  n_phys):
    nb = x.shape[0]
    bd256, bd384, _, _, perm = consts
    outs = _proj_call(x, p['g'], p['w32'], *tabs, p['qn'], p['kn'], p['mqn'], bd256, bd384,
                      tm=nb, t_tiles=1, precise=True)
    q, nsa, win, gates, hq, hf, hi, u, qm, z = outs
    cc = _cmp_pages_call(cache_pages, p['pe2'], perm, p['wr2'], p['w2e'], pp=64, n_pages=n_phys,
                         page_off=l * n_phys)
    idx, o_c = _scmp_call(page_table, q, cc.reshape(n_phys, 512))
    r3 = lambda a: a.reshape(nb, 1, -1)
    o_nsa = _ssel_call(page_table, idx, cache_pairs, r3(q), r3(nsa), win_cache, r3(win), r3(gates),
                       o_c.reshape(nb, 8, HEAD_DIM), layer=l, n_phys=n_phys)
    o_hg, o_mem, s_new = _srest_call(r3(hq), r3(hf), r3(hi), r3(qm), state, cmem,
                                     p['la'], p['lc'], p['oml'], p['on'], layer=l)
    y = _sfinal_call(x, o_nsa.reshape(nb, 256), o_hg.reshape(nb, 256), u, pool_buf, o_mem.reshape(nb, 256), z,
                     p['pw32'], p['ps'], p['wo32'], layer=l)
    return y, nsa, win, s_new.reshape(nb, N_HEADS, HEAD_DIM, HEAD_DIM), u


def kernel(x_prompt, x_sample, mem_prompt, cache_nsa, cache_nsa_win, state_hgrn, state_pool, cache_mem,
           page_table, norm_g, w_in, w_out, nsa_qn, nsa_kn, cmp_pe, cmp_w1, cmp_w2, hg_lb, hg_on,
           pool_w, pool_scale, mem_norm, w_mem_kv, mem_qn, mem_kn):
    depth = w_in.shape[0]
    b, t, _ = x_prompt.shape
    nb = x_sample.shape[0]
    past_len = page_table.shape[1] * PAGE_SIZE
    n_phys = cache_nsa.shape[1]
    wb = cache_nsa_win.shape[2]

    lbs = jnp.cumsum(jax.nn.softmax(hg_lb.astype(jnp.float32), axis=0), axis=0)
    lbs = lbs - lbs[0:1]
    d_np, mk_np = _hgrn_consts()
    consts = (_block_diag_ones(256), _block_diag_ones(384), jnp.asarray(d_np, BF16), jnp.asarray(mk_np, F32),
              _pair_perm())
    tabs_p = _rope_tables(jnp.arange(t))
    tabs_s = tuple(jnp.broadcast_to(a, (nb, a.shape[1])) for a in _rope_tables(past_len + jnp.arange(1)))

    cache_t = jnp.transpose(cache_nsa, (0, 1, 3, 4, 5, 2))
    cache_pages = cache_t.reshape(depth * n_phys, 4 * HEAD_DIM, PAGE_SIZE)
    cache_pairs = cache_t.reshape(depth * n_phys * 2, 2 * HEAD_DIM, PAGE_SIZE)
    win_cache = jnp.transpose(cache_nsa_win, (0, 1, 3, 4, 5, 2)).reshape(depth * nb, 2 * HEAD_DIM, wb)
    state = state_hgrn.reshape(depth * nb, GROUP_W, HEAD_DIM)
    pool_buf = jnp.transpose(state_pool, (0, 2, 1, 3))
    cmem = jnp.transpose(cache_mem, (0, 1, 3, 4, 5, 2)).reshape(depth * nb, 2 * GROUP_W, N_MEM)

    xp, xs = x_prompt, x_sample.reshape(nb, D_MODEL)
    acc = [[] for _ in range(9)]
    for l in range(depth):
        p = _layer_params(l, norm_g, w_in, w_out, nsa_qn, nsa_kn, cmp_pe, cmp_w1, cmp_w2, lbs, hg_on, pool_w,
                          pool_scale, mem_norm, w_mem_kv, mem_qn, mem_kn)
        xp, a, bw, c, d, e = _prompt_layer(xp, mem_prompt, p, tabs_p, consts)
        xs, sa, sw, sc, sd = _sample_layer(l, xs, p, tabs_s, consts, cache_pages, cache_pairs, win_cache, state,
                                           pool_buf, cmem, page_table, n_phys)
        new_win_s = jnp.concatenate([cache_nsa_win[l][:, 1:], sw.reshape(nb, 1, 2, 1, HEAD_DIM)], axis=1)[:, -wb:]
        new_pool_s = jnp.concatenate([state_pool[l][:, 1:], sd.reshape(nb, 1, GROUP_W)], axis=1)
        for lst, val in zip(acc, (a, bw, c, d, e, sa.reshape(nb, 1, 4, 1, HEAD_DIM), new_win_s, sc, new_pool_s)):
            lst.append(val)
    return (xp, xs.reshape(nb, 1, D_MODEL)) + tuple(jnp.stack(v) for v in acc)
```

```python
import functools

import numpy as np
import jax
import jax.numpy as jnp
from jax import lax
from jax.experimental import pallas as pl
from jax.experimental.pallas import tpu as pltpu

D_MODEL = 1024
HEAD_DIM = 64
GROUP_W = 256
N_HEADS = 4
L_CMP = 32
L_SLC = 64
N_SEL = 16
WINDOW = 512
CMP_HID = 256
POOL_SIZES = (2, 4, 8, 16)
POOL_BUF = 15
N_MEM = 256
PAGE_SIZE = 128
ROPE_THETA = 10000.0
EPS = 1e-6
QK_SCALE = HEAD_DIM ** -0.5

PROJ_W = 3072
HG_CHUNK = 256
NSA_TQ = 256
NEG = -1e30
VMEM_LIMIT = 56 * 1024 * 1024

F32 = jnp.float32
BF16 = jnp.bfloat16
HI = lax.Precision.HIGHEST


def _cparams(sem):
    return pltpu.CompilerParams(dimension_semantics=sem, vmem_limit_bytes=VMEM_LIMIT)


def _bdot(a, b):
    return jnp.dot(a.astype(BF16), b.astype(BF16), preferred_element_type=F32)


def _bdot_nt(a, b):
    return lax.dot_general(a.astype(BF16), b.astype(BF16), (((1,), (1,)), ((), ())), preferred_element_type=F32)


def _hdot(a, b):
    return jnp.dot(a, b, precision=HI, preferred_element_type=F32)


def _hdot_nt(a, b):
    return lax.dot_general(a, b, (((1,), (1,)), ((), ())), precision=HI, preferred_element_type=F32)


def _dot2(a, b16):
    hi = a.astype(BF16)
    lo = (a - hi.astype(F32)).astype(BF16)
    return jnp.dot(hi, b16, preferred_element_type=F32) + jnp.dot(lo, b16, preferred_element_type=F32)


def _dot2_left(a16, b):
    hi = b.astype(BF16)
    lo = (b - hi.astype(F32)).astype(BF16)
    return jnp.dot(a16, hi, preferred_element_type=F32) + jnp.dot(a16, lo, preferred_element_type=F32)


def _seg_rms(x, bd16, gain):
    ms = _dot2(x * x, bd16) * (1.0 / HEAD_DIM)
    return x * lax.rsqrt(ms + EPS) * gain


def _rope(x, cos, sin_signed):
    w = x.shape[1]
    lane = lax.broadcasted_iota(jnp.int32, x.shape, 1)
    first = (lane & 32) == 0
    rot = jnp.where(first, pltpu.roll(x, w - 32, 1), pltpu.roll(x, 32, 1))
    return x * cos + rot * sin_signed


def _sigmoid(x):
    return jax.nn.sigmoid(x)


def _lane_head(shape):
    return lax.broadcasted_iota(jnp.int32, shape, 1) // HEAD_DIM


def _proj_kernel(x_ref, g_ref, w_ref, cq_ref, sq_ref, ckv_ref, skv_ref, qn_ref, kn_ref, mqn_ref, bd256_ref, bd384_ref,
                 q_out, nsa_out, win_out, gate_out, hq_out, hf_out, hi_out, u_out, qm_out, z_out, *, precise):
    x = x_ref[...]
    ms = jnp.mean(x * x, axis=-1, keepdims=True)
    h = x * lax.rsqrt(ms + EPS) * g_ref[...]
    proj = _hdot(h, w_ref[...]) if precise else _bdot(h, w_ref[...])
    bd256 = bd256_ref[...]
    q = _rope(_seg_rms(proj[:, 0:256], bd256, qn_ref[...]), cq_ref[...], sq_ref[...])
    q_out[...] = q
    kv = proj[:, 256:640]
    kvn = _rope(_seg_rms(kv, bd384_ref[...], kn_ref[...]), ckv_ref[...], skv_ref[...])
    lane = lax.broadcasted_iota(jnp.int32, kv.shape, 1)
    is_key = ((lane // HEAD_DIM) & 1) == 0
    kv = jnp.where(is_key, kvn, kv)
    nsa_out[...] = kv[:, 0:256]
    win_out[...] = kv[:, 256:384]
    gate_out[...] = _sigmoid(proj[:, 640:768])
    hq_out[...] = proj[:, 768:1024]
    hf_out[...] = proj[:, 1024:1280]
    hi_out[...] = proj[:, 1280:1536]
    u_out[...] = proj[:, 1536:1792]
    qm_out[...] = _seg_rms(proj[:, 1792:2048], bd256, mqn_ref[...])
    z_out[...] = proj[:, 2048:3072]


def _proj_call(x, g, w, cq, sq, ckv, skv, qn, kn, mqn, bd256, bd384, *, tm, t_tiles, precise):
    n = x.shape[0]
    row = lambda wd: pl.BlockSpec((tm, wd), lambda i: (i, 0))
    tab = lambda wd: pl.BlockSpec((tm, wd), lambda i: (i % t_tiles, 0))
    full = lambda a: pl.BlockSpec(a.shape, lambda i: (0, 0))
    widths = (256, 256, 128, 128, 256, 256, 256, 256, 256, 1024)
    return pl.pallas_call(
        functools.partial(_proj_kernel, precise=precise),
        grid=(n // tm,),
        in_specs=[row(D_MODEL), full(g), full(w), tab(256), tab(256), tab(384), tab(384),
                  full(qn), full(kn), full(mqn), full(bd256), full(bd384)],
        out_specs=[row(wd) for wd in widths],
        out_shape=[jax.ShapeDtypeStruct((n, wd), F32) for wd in widths],
        compiler_params=_cparams(("parallel",)),
    )(x, g, w, cq, sq, ckv, skv, qn, kn, mqn, bd256, bd384)


def _cmp_kernel(x_ref, pe_ref, w1_ref, w2_ref, o_ref):
    x = x_ref[...] + pe_ref[...]
    h = _bdot(x, w1_ref[...])
    h = h * _sigmoid(h)
    o_ref[...] = _bdot(h, w2_ref[...])


def _cmp_call(x, pe, w1, w2, *, tr, n_rows, row_off):
    off = row_off // tr
    full = lambda a: pl.BlockSpec(a.shape, lambda i: (0, 0))
    return pl.pallas_call(
        _cmp_kernel,
        grid=(n_rows // tr,),
        in_specs=[pl.BlockSpec((tr, x.shape[1]), lambda i: (i + off, 0)), full(pe), full(w1), full(w2)],
        out_specs=pl.BlockSpec((tr, 128), lambda i: (i, 0)),
        out_shape=jax.ShapeDtypeStruct((n_rows, 128), F32),
        compiler_params=_cparams(("parallel",)),
    )(x, pe, w1, w2)


def _cmp_pages_kernel(pg_ref, pe_ref, perm_ref, wr_ref, w2_ref, o_ref, xs_scr):
    npg = pg_ref.shape[0]
    pe2 = pe_ref[...]
    perm = perm_ref[...]

    for i in range(npg // 2):
        m2 = jnp.concatenate([pg_ref[2 * i], pg_ref[2 * i + 1]], axis=1) + pe2
        xp = lax.dot_general(perm, m2.astype(BF16), (((1,), (1,)), ((), ())), preferred_element_type=F32)
        xs_scr[:, 8 * i:8 * i + 8, :] = xp.reshape(L_CMP, 8, 128)
    acc = jnp.zeros((npg * (PAGE_SIZE // L_CMP), 2 * CMP_HID), F32)
    for r2 in range(L_CMP // 2):
        slab = jnp.concatenate([xs_scr[2 * r2], xs_scr[2 * r2 + 1]], axis=1)
        acc = acc + _bdot(slab, wr_ref[r2])
    h = acc * _sigmoid(acc)
    o_ref[...] = _bdot(h, w2_ref[...])


def _cmp_pages_call(pages, pe2, perm, wr2, w2, *, pp, n_pages, page_off):
    off = page_off // pp
    n_out = pp * (PAGE_SIZE // L_CMP)
    full = lambda a: pl.BlockSpec(a.shape, lambda i: (0,) * a.ndim)
    return pl.pallas_call(
        _cmp_pages_kernel,
        grid=(n_pages // pp,),
        in_specs=[pl.BlockSpec((pp, 128, PAGE_SIZE), lambda i: (i + off, 0, 0)),
                  full(pe2), full(perm), full(wr2), full(w2)],
        out_specs=pl.BlockSpec((n_out, 128), lambda i: (i, 0)),
        out_shape=jax.ShapeDtypeStruct((n_pages * (PAGE_SIZE // L_CMP), 128), F32),
        scratch_shapes=[pltpu.VMEM((L_CMP, n_out, 128), F32)],
        compiler_params=_cparams(("parallel",)),
    )(pages, pe2, perm, wr2, w2)


def _pair_perm():
    a = np.zeros((L_CMP, 8, 2, PAGE_SIZE), np.float32)
    for r in range(L_CMP):
        for pq in range(2):
            for nb in range(PAGE_SIZE // L_CMP):
                a[r, 4 * pq + nb, pq, L_CMP * nb + r] = 1.0
    return jnp.asarray(a.reshape(L_CMP * 8, 2 * PAGE_SIZE), BF16)


def _topk_mask(score, k, idx_f, axis=1):
    sel = jnp.zeros(score.shape, jnp.bool_)
    s = score
    big = jnp.float32(1e9)
    picks = []
    for _ in range(k):
        m = jnp.max(s, axis=axis, keepdims=True)
        cand = jnp.where(s == m, idx_f, big)
        pick = jnp.min(cand, axis=axis, keepdims=True)
        hit = idx_f == pick
        sel = sel | hit
        s = jnp.where(hit, -jnp.inf, s)
        picks.append(pick)
    return sel, picks


def _flash_step(carry, k16, vt16, bias, qs_scr):
    m, l, acc = carry
    s = lax.dot_general(k16, qs_scr[...], (((1,), (1,)), ((), ())), preferred_element_type=F32)
    s = s + (jnp.concatenate([bias] * N_HEADS, axis=1) if bias.ndim == 2 else bias)
    m_new = jnp.maximum(m, jnp.max(s, axis=0, keepdims=True))
    alpha = jnp.exp(m - m_new)
    p = jnp.exp(s - m_new)
    l = alpha * l + jnp.sum(p, axis=0, keepdims=True)
    acc = alpha * acc + jnp.dot(vt16, p.astype(BF16), preferred_element_type=F32)
    return m_new, l, acc


def _nsa_kernel(q_ref, nsa_ref, win_ref, gate_ref, cc_ref, o_ref, qs_scr, vt_scr, wt_scr, *, tq):
    i = pl.program_id(1)
    t_len = nsa_ref.shape[1]
    assert WINDOW == 2 * tq and t_len % tq == 0

    @pl.when(i == 0)
    def _():
        for c in range(t_len // tq):
            sl = slice(c * tq, (c + 1) * tq)
            vt_scr[c] = nsa_ref[0, sl, 128:256].T.astype(BF16)
            wt_scr[c] = win_ref[0, sl, :].T.astype(BF16)

    t0 = i * tq
    r4 = N_HEADS * tq
    q = q_ref[0] * QK_SCALE
    for h in range(N_HEADS):
        qs_scr[h * tq:(h + 1) * tq, :] = q[:, h * HEAD_DIM:(h + 1) * HEAD_DIM].astype(BF16)
    t_pos = t0 + lax.broadcasted_iota(jnp.int32, (1, tq), 1)
    t_pos4 = t0 + lax.broadcasted_iota(jnp.int32, (1, r4), 1) % tq

    n_cmp = HEAD_DIM
    cc = cc_ref[0]
    kcc16 = cc[0:n_cmp, 0:HEAD_DIM].astype(BF16)
    vcct16 = cc.T[HEAD_DIM:2 * HEAD_DIM, 0:n_cmp].astype(BF16)
    done = (lax.broadcasted_iota(jnp.int32, (n_cmp, r4), 0) + 1) * L_CMP - 1 <= t_pos4
    s = lax.dot_general(kcc16, qs_scr[...], (((1,), (1,)), ((), ())), preferred_element_type=F32)
    s = jnp.where(done, s, NEG)
    m = jnp.max(s, axis=0, keepdims=True)
    m = jnp.where(m < 0.5 * NEG, 0.0, m)
    p = jnp.where(done, jnp.exp(s - m), 0.0)
    p = p / jnp.maximum(jnp.sum(p, axis=0, keepdims=True), 1e-30)
    o_c = jnp.dot(vcct16, p.astype(BF16), preferred_element_type=F32)
    imp = p[:, 0:tq] + p[:, tq:2 * tq] + p[:, 2 * tq:3 * tq] + p[:, 3 * tq:4 * tq]

    n_idx = lax.broadcasted_iota(jnp.int32, (n_cmp, tq), 0)
    imp2 = imp + pltpu.roll(imp, n_cmp - 1, 0)
    blk_t = t_pos // L_SLC
    j = n_idx // 2
    valid = ((n_idx & 1) == 0) & (j < t_len // L_SLC)
    avail = valid & (j <= blk_t)
    forced = (j == 0) | (j == blk_t) | (j == blk_t - 1)
    score = jnp.where(avail, jnp.where(forced, jnp.inf, imp2), -jnp.inf)
    sel, _ = _topk_mask(score, N_SEL, n_idx.astype(F32), axis=0)
    sel16 = jnp.where(sel & avail, 1.0, 0.0).astype(BF16)

    key_l = lax.broadcasted_iota(jnp.int32, (tq, tq), 0)
    qry_l = lax.broadcasted_iota(jnp.int32, (tq, tq), 1)
    causal = jnp.where(key_l <= qry_l, 0.0, NEG)
    tail = jnp.where(key_l > qry_l, 0.0, NEG)

    def sel_bias(c):
        ek = c * tq + lax.broadcasted_iota(jnp.int32, (tq, n_cmp), 0)
        en = lax.broadcasted_iota(jnp.int32, (tq, n_cmp), 1)
        expand = jnp.where(en == 2 * (ek // L_SLC), 1.0, 0.0).astype(BF16)
        msel = jnp.dot(expand, sel16, preferred_element_type=F32)
        return (msel - 1.0) * (-NEG)

    def chunk(ref, vt, c, lo):
        start = pl.multiple_of(c * tq, tq)
        return ref[0, pl.ds(start, tq), lo:lo + HEAD_DIM].astype(BF16), vt[c, HEAD_DIM:2 * HEAD_DIM, :]

    init = (jnp.full((1, r4), NEG, F32), jnp.zeros((1, r4), F32), jnp.zeros((HEAD_DIM, r4), F32))

    def sel_body(c, carry):
        k16, vt16 = chunk(nsa_ref, vt_scr, c, 128)
        return _flash_step(carry, k16, vt16, sel_bias(c), qs_scr)

    st_s = lax.fori_loop(0, i, sel_body, init)
    k16, vt16 = chunk(nsa_ref, vt_scr, i, 128)
    st_s = _flash_step(st_s, k16, vt16, sel_bias(i) + causal, qs_scr)

    k16, vt16 = chunk(win_ref, wt_scr, i, 0)
    st_w = _flash_step(init, k16, vt16, causal, qs_scr)
    k16, vt16 = chunk(win_ref, wt_scr, jnp.maximum(i - 1, 0), 0)
    st_w = _flash_step(st_w, k16, vt16, jnp.where(i >= 1, 0.0, NEG), qs_scr)
    k16, vt16 = chunk(win_ref, wt_scr, jnp.maximum(i - 2, 0), 0)
    st_w = _flash_step(st_w, k16, vt16, tail + jnp.where(i >= 2, 0.0, NEG), qs_scr)

    gt = gate_ref[0].T
    g_c, g_s, g_w = (jnp.concatenate([gt[3 * h + br:3 * h + br + 1] for h in range(N_HEADS)], axis=1)
                     for br in range(3))
    o_s = st_s[2] / jnp.maximum(st_s[1], 1e-30)
    o_w = st_w[2] / jnp.maximum(st_w[1], 1e-30)
    o = g_c * o_c + g_s * o_s + g_w * o_w
    o_ref[0] = jnp.concatenate([o[:, h * tq:(h + 1) * tq] for h in range(N_HEADS)], axis=0).T


def _nsa_call(q, nsa, win, gates, cc, *, tq):
    b, t, _ = q.shape
    return pl.pallas_call(
        functools.partial(_nsa_kernel, tq=tq),
        grid=(b, t // tq),
        in_specs=[pl.BlockSpec((1, tq, 256), lambda bi, i: (bi, i, 0)),
                  pl.BlockSpec((1, t, 256), lambda bi, i: (bi, 0, 0)),
                  pl.BlockSpec((1, t, 128), lambda bi, i: (bi, 0, 0)),
                  pl.BlockSpec((1, tq, 128), lambda bi, i: (bi, i, 0)),
                  pl.BlockSpec((1, 128, 128), lambda bi, i: (bi, 0, 0))],
        out_specs=pl.BlockSpec((1, tq, 256), lambda bi, i: (bi, i, 0)),
        out_shape=jax.ShapeDtypeStruct((b, t, 256), F32),
        scratch_shapes=[pltpu.VMEM((N_HEADS * tq, HEAD_DIM), BF16), pltpu.VMEM((t // tq, 128, tq), BF16),
                        pltpu.VMEM((t // tq, 128, tq), BF16)],
        compiler_params=_cparams(("parallel", "arbitrary")),
    )(q, nsa, win, gates, cc)


def _hgrn_consts(c=HG_CHUNK):
    t = np.arange(c)[:, None]
    k = np.arange(c)[None, :]
    masks = []
    m = 1
    while m < c:
        upper = (t % (2 * m)) >= m
        masks.append(upper & ((k // (2 * m)) == t // (2 * m)) & ((k % (2 * m)) < m))
        m *= 2
    masks.append(t == k)
    return np.stack(masks).astype(np.float32)


def _block_row_bcast(p, m):
    c, w = p.shape
    if 2 * m >= 8:
        p3 = p.reshape(c // (2 * m), 2 * m, w)
        return jnp.broadcast_to(p3[:, m - 1:m, :], p3.shape).reshape(c, w)
    off = lax.broadcasted_iota(jnp.int32, p.shape, 0) % (2 * m)
    out = p
    for o in range(2 * m):
        if o != m - 1:
            out = jnp.where(off == o, pltpu.roll(p, (o - (m - 1)) % c, 0), out)
    return out


def _log_decay(fr, la, lc):
    ls = jnp.minimum(fr, 0.0) - jnp.log1p(jnp.exp(-jnp.abs(fr)))
    b = lc + ls
    return jnp.maximum(la, b) + jnp.log1p(jnp.exp(-jnp.abs(la - b)))


def _hgrn_kernel(hq_ref, hf_ref, hi_ref, la_ref, lc_ref, oml_ref, on_ref, mk_ref, bd_ref,
                 o_ref, st_ref, st_scr):
    c = HG_CHUNK
    jt = pl.program_id(1)

    @pl.when(jt == 0)
    def _():
        st_scr[...] = jnp.zeros_like(st_scr)

    q = hq_ref[0]
    fr = hf_ref[0]
    v = hi_ref[0]
    logf = _log_decay(fr, la_ref[...], lc_ref[...])
    k = oml_ref[...] * _sigmoid(-fr)
    row = lax.broadcasted_iota(jnp.int32, (c, GROUP_W), 0)
    pre = logf
    decays = []
    m = 1
    while m < c:
        mid = _block_row_bcast(pre, m)
        upper = (row % (2 * m)) >= m
        decays.append(jnp.exp(jnp.where(upper, pre, mid - pre)))
        pre = pre + jnp.where(upper, mid, 0.0)
        m *= 2
    gc = pre
    gl = gc[c - 1:c, :]
    lane_h = _lane_head((c, GROUP_W))
    n_lvl = mk_ref.shape[0]
    a_tot = [jnp.zeros((c, c), F32) for _ in range(N_HEADS)]
    for lv in range(n_lvl):
        qt = q * decays[lv] if lv < len(decays) else q
        kt = (k * decays[lv] if lv < len(decays) else k).astype(BF16)
        qs = jnp.concatenate([jnp.where(lane_h == h, qt, 0.0).astype(BF16) for h in range(N_HEADS)], axis=0)
        a_l = lax.dot_general(qs, kt, (((1,), (1,)), ((), ())), preferred_element_type=F32)
        mk = mk_ref[lv]
        for h in range(N_HEADS):
            a_tot[h] = a_tot[h] + mk * a_l[h * c:(h + 1) * c]
    a_all = jnp.concatenate(a_tot, axis=0)
    r = _bdot(a_all, v)
    o = jnp.zeros((c, GROUP_W), F32)
    for h in range(N_HEADS):
        o = o + jnp.where(lane_h == h, r[h * c:(h + 1) * c], 0.0)
    st = st_scr[...]
    o = o + _bdot_nt(q * jnp.exp(gc), st)
    khat = k * jnp.exp(gl - gc)
    upd = jnp.dot(v.T.astype(BF16), khat.astype(BF16), preferred_element_type=F32)
    row_h = lax.broadcasted_iota(jnp.int32, (GROUP_W, GROUP_W), 0) // HEAD_DIM
    col_h = _lane_head((GROUP_W, GROUP_W))
    st_new = jnp.exp(gl) * st + jnp.where(row_h == col_h, upd, 0.0)
    st_scr[...] = st_new
    st_ref[0] = st_new
    o_ref[0] = _seg_rms(o, bd_ref[...], on_ref[...])


def _hgrn_call(hq, hf, hi, la, lc, oml, on, mk, bd256):
    b, t, _ = hq.shape
    c = HG_CHUNK
    row = pl.BlockSpec((1, c, GROUP_W), lambda bi, j: (bi, j, 0))
    full2 = lambda a: pl.BlockSpec(a.shape, lambda bi, j: (0, 0))
    return pl.pallas_call(
        _hgrn_kernel,
        grid=(b, t // c),
        in_specs=[row, row, row, full2(la), full2(lc), full2(oml), full2(on),
                  pl.BlockSpec(mk.shape, lambda bi, j: (0, 0, 0)), full2(bd256)],
        out_specs=[row, pl.BlockSpec((1, GROUP_W, GROUP_W), lambda bi, j: (bi, 0, 0))],
        out_shape=[jax.ShapeDtypeStruct((b, t, GROUP_W), F32), jax.ShapeDtypeStruct((b, GROUP_W, GROUP_W), F32)],
        scratch_shapes=[pltpu.VMEM((GROUP_W, GROUP_W), F32)],
        compiler_params=_cparams(("parallel", "arbitrary")),
    )(hq, hf, hi, la, lc, oml, on, mk, bd256)


def _memkv_kernel(m_ref, g_ref, w_ref, kn_ref, bd_ref, o_ref):
    x = m_ref[0]
    ms = jnp.mean(x * x, axis=-1, keepdims=True)
    h = x * lax.rsqrt(ms + EPS) * g_ref[...]
    kv = _bdot(h, w_ref[...])
    o_ref[0, :, 0:GROUP_W] = _seg_rms(kv[:, 0:GROUP_W], bd_ref[...], kn_ref[...])
    o_ref[0, :, GROUP_W:2 * GROUP_W] = kv[:, GROUP_W:2 * GROUP_W]


def _memkv_call(mem, g, w, kn, bd256):
    b = mem.shape[0]
    full = lambda a: pl.BlockSpec(a.shape, lambda i: (0, 0))
    return pl.pallas_call(
        _memkv_kernel,
        grid=(b,),
        in_specs=[pl.BlockSpec((1, N_MEM, D_MODEL), lambda i: (i, 0, 0)), full(g), full(w), full(kn), full(bd256)],
        out_specs=pl.BlockSpec((1, N_MEM, 2 * GROUP_W), lambda i: (i, 0, 0)),
        out_shape=jax.ShapeDtypeStruct((b, N_MEM, 2 * GROUP_W), F32),
        compiler_params=_cparams(("parallel",)),
    )(mem, g, w, kn, bd256)


def _final_kernel(x_ref, onsa_ref, ohg_ref, u_ref, halo_ref, qm_ref, z_ref, mkv_ref, pw_ref, ps_ref, wo_ref,
                  y_ref, ext_scr, *, tm):
    jt = pl.program_id(1)
    u = u_ref[0]
    halo = halo_ref[0]
    ext_scr[0:16, :] = jnp.where(jt == 0, 0.0, halo)
    ext_scr[16:16 + tm, :] = u
    pos = jt * tm + lax.broadcasted_iota(jnp.int32, (tm, 1), 0)
    lane_g = _lane_head((tm, GROUP_W))
    run = u
    pooled = jnp.zeros((tm, GROUP_W), F32)
    nxt = 1
    for gi, wsz in enumerate(POOL_SIZES):
        for sh in range(nxt, wsz):
            run = run + ext_scr[16 - sh:16 - sh + tm, :]
        nxt = wsz
        cnt = jnp.minimum(wsz, pos + 1).astype(F32)
        pooled = jnp.where(lane_g == gi, run / cnt, pooled)
    o_pool = _bdot(pooled - u, pw_ref[...]) * ps_ref[...]

    qm = qm_ref[0] * QK_SCALE
    mkv = mkv_ref[0]
    heads = []
    for h in range(N_HEADS):
        sl = slice(h * HEAD_DIM, (h + 1) * HEAD_DIM)
        s = _bdot_nt(qm[:, sl], mkv[:, sl])
        s = s - jnp.max(s, axis=1, keepdims=True)
        p = jnp.exp(s)
        p = p / jnp.sum(p, axis=1, keepdims=True)
        heads.append(_bdot(p, mkv[:, GROUP_W + h * HEAD_DIM:GROUP_W + (h + 1) * HEAD_DIM]))
    o_mem = jnp.concatenate(heads, axis=1)

    z = z_ref[0]
    y = jnp.concatenate([onsa_ref[0], ohg_ref[0], o_pool, o_mem], axis=1) * (z * _sigmoid(z))
    y_ref[0] = x_ref[0] + _bdot(y, wo_ref[...])


def _final_call(x, onsa, ohg, u, qm, z, mkv, pw, ps, wo, *, tm):
    b, t, _ = x.shape
    row = lambda wd: pl.BlockSpec((1, tm, wd), lambda bi, j: (bi, j, 0))
    full2 = lambda a: pl.BlockSpec(a.shape, lambda bi, j: (0, 0))
    hb = tm // 16
    return pl.pallas_call(
        functools.partial(_final_kernel, tm=tm),
        grid=(b, t // tm),
        in_specs=[row(D_MODEL), row(GROUP_W), row(GROUP_W), row(GROUP_W),
                  pl.BlockSpec((1, 16, GROUP_W), lambda bi, j: (bi, jnp.maximum(j * hb - 1, 0), 0)),
                  row(GROUP_W), row(D_MODEL),
                  pl.BlockSpec((1, N_MEM, 2 * GROUP_W), lambda bi, j: (bi, 0, 0)),
                  full2(pw), full2(ps), full2(wo)],
        out_specs=row(D_MODEL),
        out_shape=jax.ShapeDtypeStruct((b, t, D_MODEL), F32),
        scratch_shapes=[pltpu.VMEM((tm + 16, GROUP_W), F32)],
        compiler_params=_cparams(("parallel", "arbitrary")),
    )(x, onsa, ohg, u, u, qm, z, mkv, pw, ps, wo)


def _head_rows(row):
    r8 = lax.broadcasted_iota(jnp.int32, (8, GROUP_W), 0)
    qm = jnp.where(r8 == _lane_head((8, GROUP_W)), jnp.broadcast_to(row, (8, GROUP_W)), 0.0)
    return qm[:, 0:64] + qm[:, 64:128] + qm[:, 128:192] + qm[:, 192:256]


def _scmp_kernel(pt_ref, q_ref, cc_hbm, idx_ref, oc_ref, g_scr, a_scr, b_scr, sem, *, nb, npg):
    def copy(n):
        b = n // npg
        pg = pt_ref[b, n - b * npg]
        return pltpu.make_async_copy(cc_hbm.at[pl.ds(pg, 1)], g_scr.at[pl.ds(n, 1)], sem)

    def start(n, c):
        copy(n).start()
        return c

    def wait(n, c):
        copy(n).wait()
        return c

    lax.fori_loop(0, nb * npg, start, 0)
    lax.fori_loop(0, nb * npg, wait, 0)

    def per_batch(b, c):
        gb = g_scr[pl.ds(pl.multiple_of(b * npg, npg), npg), :]
        qh = _head_rows(q_ref[pl.ds(b, 1), :]) * QK_SCALE
        ss = [_hdot_nt(qh, gb[:, r * 128:r * 128 + HEAD_DIM]) for r in range(4)]
        m = ss[0].max(axis=1, keepdims=True)
        for r in range(1, 4):
            m = jnp.maximum(m, ss[r].max(axis=1, keepdims=True))
        ps = [jnp.exp(s - m) for s in ss]
        l = ps[0].sum(axis=1, keepdims=True)
        for r in range(1, 4):
            l = l + ps[r].sum(axis=1, keepdims=True)
        inv = 1.0 / jnp.maximum(l, 1e-30)
        ps = [p * inv for p in ps]
        oc = _hdot(ps[0], gb[:, HEAD_DIM:128])
        for r in range(1, 4):
            oc = oc + _hdot(ps[r], gb[:, r * 128 + HEAD_DIM:(r + 1) * 128])
        oc_ref[pl.ds(pl.multiple_of(b * 8, 8), 8), :] = oc
        hmask = lax.broadcasted_iota(jnp.int32, ss[0].shape, 0) < N_HEADS
        imps = [jnp.sum(jnp.where(hmask, p, 0.0), axis=0, keepdims=True) for p in ps]
        a_scr[pl.ds(b, 1), :] = imps[0] + imps[1]
        b_scr[pl.ds(b, 1), :] = imps[2] + imps[3]
        return c

    lax.fori_loop(0, nb, per_batch, 0)

    score = jnp.concatenate([a_scr[...], b_scr[...]], axis=1)
    lane = lax.broadcasted_iota(jnp.int32, score.shape, 1)
    jidx = jnp.where(lane < npg, 2 * lane, 2 * (lane - npg) + 1)
    forced = (jidx == 0) | (jidx == 2 * npg - 1)
    score = jnp.where(forced, jnp.inf, score)
    _, picks = _topk_mask(score, N_SEL - 1, jidx.astype(F32))
    out_lane = lax.broadcasted_iota(jnp.int32, (nb, 128), 1)
    out = jnp.zeros((nb, 128), jnp.int32)
    for r, pk in enumerate(picks):
        out = jnp.where(out_lane == r, pk.astype(jnp.int32), out)
    idx_ref[...] = out


def _scmp_call(page_table, q, cc_rows):
    nb, npg = page_table.shape
    return pl.pallas_call(
        functools.partial(_scmp_kernel, nb=nb, npg=npg),
        grid_spec=pltpu.PrefetchScalarGridSpec(
            num_scalar_prefetch=1,
            grid=(1,),
            in_specs=[pl.BlockSpec(q.shape, lambda i, pt: (0, 0)), pl.BlockSpec(memory_space=pl.ANY)],
            out_specs=[pl.BlockSpec((nb, 128), lambda i, pt: (0, 0)), pl.BlockSpec((nb * 8, HEAD_DIM), lambda i, pt: (0, 0))],
            scratch_shapes=[pltpu.VMEM((nb * npg, 512), F32), pltpu.VMEM((nb, 128), F32), pltpu.VMEM((nb, 128), F32),
                            pltpu.SemaphoreType.DMA(())],
        ),
        out_shape=[jax.ShapeDtypeStruct((nb, 128), jnp.int32), jax.ShapeDtypeStruct((nb * 8, HEAD_DIM), F32)],
        compiler_params=_cparams(("arbitrary",)),
    )(page_table, q, cc_rows)


def _ssel_kernel(pt_ref, idx_ref, pairs_hbm, wc_hbm, q_ref, new_ref, wnew_ref, gate_ref, oc_ref, o_ref,
                 blk_buf, wc_buf, blk_sem, wc_sem, *, n_gather, nb, pg_off, wc_off):
    def copies(b, slot):
        out = []
        for r in range(n_gather):
            src = (pg_off + pt_ref[b, idx_ref[b, r] // 2]) * 2 + 1
            out.append(pltpu.make_async_copy(pairs_hbm.at[src], blk_buf.at[slot, r], blk_sem.at[slot]))
        out.append(pltpu.make_async_copy(wc_hbm.at[wc_off + b], wc_buf.at[slot], wc_sem.at[slot]))
        return out

    for cp in copies(0, 0):
        cp.start()
    lane_half = lax.broadcasted_iota(jnp.int32, (8, PAGE_SIZE), 1) // L_SLC

    def per_seq(b, carry):
        slot = b % 2

        @pl.when(b + 1 < nb)
        def _():
            for cp in copies(b + 1, 1 - slot):
                cp.start()

        for cp in copies(b, slot):
            cp.wait()
        qh = _head_rows(q_ref[pl.ds(b, 1), :]) * QK_SCALE
        new = new_ref[pl.ds(b, 1), :]
        s_n = jnp.sum(qh * new[:, 128:192], axis=1, keepdims=True)
        scores = []
        m = s_n
        for r in range(n_gather):
            half = idx_ref[b, r] % 2
            s = _hdot(qh, blk_buf[slot, r, 0:HEAD_DIM, :])
            s = jnp.where(lane_half == half, s, NEG)
            scores.append(s)
            m = jnp.maximum(m, s.max(axis=1, keepdims=True))
        p_n = jnp.exp(s_n - m)
        l = p_n
        acc = p_n * new[:, 192:256]
        for r in range(n_gather):
            p = jnp.exp(scores[r] - m)
            l = l + p.sum(axis=1, keepdims=True)
            acc = acc + _hdot_nt(p, blk_buf[slot, r, HEAD_DIM:2 * HEAD_DIM, :])
        o_s = acc / jnp.maximum(l, 1e-30)
        wnew = wnew_ref[pl.ds(b, 1), :]
        s_w = _hdot(qh, wc_buf[slot, 0:HEAD_DIM, :])
        keep = lax.broadcasted_iota(jnp.int32, s_w.shape, 1) >= s_w.shape[1] + 1 - WINDOW
        s_w = jnp.where(keep, s_w, NEG)
        s_wn = jnp.sum(qh * wnew[:, 0:HEAD_DIM], axis=1, keepdims=True)
        m_w = jnp.maximum(s_w.max(axis=1, keepdims=True), s_wn)
        p_w = jnp.where(keep, jnp.exp(s_w - m_w), 0.0)
        p_wn = jnp.exp(s_wn - m_w)
        l_w = p_w.sum(axis=1, keepdims=True) + p_wn
        o_w = (_hdot_nt(p_w, wc_buf[slot, HEAD_DIM:128, :]) + p_wn * wnew[:, HEAD_DIM:128]) / jnp.maximum(l_w, 1e-30)
        o_c = oc_ref[pl.ds(pl.multiple_of(b * 8, 8), 8), :]
        g = gate_ref[pl.ds(b, 1), :]
        outs = []
        for h in range(N_HEADS):
            outs.append(g[:, 3 * h:3 * h + 1] * o_c[h:h + 1] + g[:, 3 * h + 1:3 * h + 2] * o_s[h:h + 1]
                        + g[:, 3 * h + 2:3 * h + 3] * o_w[h:h + 1])
        o_ref[pl.ds(b, 1), :] = jnp.concatenate(outs, axis=1)
        return carry

    lax.fori_loop(0, nb, per_seq, 0)


def _ssel_call(page_table, idx, cache_pairs, q, new_rows, win_cache, win_new, gates, o_c, *, layer, n_phys):
    nb = q.shape[0]
    n_gather = N_SEL - 1
    wb = win_cache.shape[2]
    full = lambda a: pl.BlockSpec(a.shape, lambda i, pt, ix: (0, 0))
    hbm = pl.BlockSpec(memory_space=pl.ANY)
    return pl.pallas_call(
        functools.partial(_ssel_kernel, n_gather=n_gather, nb=nb, pg_off=layer * n_phys, wc_off=layer * nb),
        grid_spec=pltpu.PrefetchScalarGridSpec(
            num_scalar_prefetch=2,
            grid=(1,),
            in_specs=[hbm, hbm, full(q), full(new_rows), full(win_new), full(gates), full(o_c)],
            out_specs=pl.BlockSpec((nb, 256), lambda i, pt, ix: (0, 0)),
            scratch_shapes=[pltpu.VMEM((2, n_gather, 128, PAGE_SIZE), F32), pltpu.VMEM((2, 128, wb), F32),
                            pltpu.SemaphoreType.DMA((2,)), pltpu.SemaphoreType.DMA((2,))],
        ),
        out_shape=jax.ShapeDtypeStruct((nb, 256), F32),
        compiler_params=_cparams(("arbitrary",)),
    )(page_table, idx, cache_pairs, win_cache, q, new_rows, win_new, gates, o_c)


def _to_col(row, eye):
    return jnp.sum(jnp.where(eye, jnp.broadcast_to(row, eye.shape), 0.0), axis=1, keepdims=True)


def _srest_kernel(hq_ref, hf_ref, hi_ref, qm_ref, st_ref, cm_ref, la_ref, lc_ref, oml_ref, on_ref,
                  ohg_ref, omem_ref, stn_ref):
    q = hq_ref[0]
    fr = hf_ref[0]
    v = hi_ref[0]
    logf = _log_decay(fr, la_ref[...], lc_ref[...])
    f = jnp.exp(logf)
    k = oml_ref[...] * _sigmoid(-fr)
    eye = (lax.broadcasted_iota(jnp.int32, (GROUP_W, GROUP_W), 0)
           == lax.broadcasted_iota(jnp.int32, (GROUP_W, GROUP_W), 1))
    qc, fc, kc = _to_col(q, eye), _to_col(f, eye), _to_col(k, eye)
    vt = jnp.concatenate([jnp.broadcast_to(v[:, h * HEAD_DIM:(h + 1) * HEAD_DIM], (HEAD_DIM, HEAD_DIM))
                          for h in range(N_HEADS)], axis=0)
    s_new = fc * st_ref[0] + kc * vt
    stn_ref[0] = s_new
    x = qc * s_new
    on = on_ref[...]
    outs = []
    for h in range(N_HEADS):
        oh = jnp.sum(x[h * HEAD_DIM:(h + 1) * HEAD_DIM], axis=0, keepdims=True)
        ms = jnp.mean(oh * oh, axis=1, keepdims=True)
        outs.append(oh * lax.rsqrt(ms + EPS) * on[:, h * HEAD_DIM:(h + 1) * HEAD_DIM])
    ohg_ref[0] = jnp.concatenate(outs, axis=1)

    cm = cm_ref[0]
    qrow = qm_ref[0] * QK_SCALE
    r8 = lax.broadcasted_iota(jnp.int32, (8, GROUP_W), 0)
    lh8 = _lane_head((8, GROUP_W))
    qbd = jnp.where(r8 == lh8, jnp.broadcast_to(qrow, (8, GROUP_W)), 0.0)
    s = _hdot(qbd, cm[0:GROUP_W])
    s = s - s.max(axis=1, keepdims=True)
    p = jnp.exp(s)
    p = p / p.sum(axis=1, keepdims=True)
    o8 = _hdot_nt(p, cm[GROUP_W:2 * GROUP_W])
    omem_ref[0] = jnp.sum(jnp.where(r8 == lh8, o8, 0.0), axis=0, keepdims=True)


def _srest_call(hq, hf, hi, qm, state, cmem, la, lc, oml, on, *, layer):
    nb = hq.shape[0]
    per_b = lambda rows, wd: pl.BlockSpec((1, rows, wd), lambda b: (b, 0, 0))
    per_lb = lambda rows, wd: pl.BlockSpec((1, rows, wd), lambda b: (layer * nb + b, 0, 0))
    full = lambda a: pl.BlockSpec(a.shape, lambda b: (0, 0))
    return pl.pallas_call(
        _srest_kernel,
        grid=(nb,),
        in_specs=[per_b(1, 256)] * 4 + [per_lb(GROUP_W, HEAD_DIM), per_lb(2 * GROUP_W, N_MEM),
                                        full(la), full(lc), full(oml), full(on)],
        out_specs=[per_b(1, 256), per_b(1, 256), per_b(GROUP_W, HEAD_DIM)],
        out_shape=[jax.ShapeDtypeStruct((nb, 1, 256), F32)] * 2 + [jax.ShapeDtypeStruct((nb, GROUP_W, HEAD_DIM), F32)],
        compiler_params=_cparams(("parallel",)),
    )(hq, hf, hi, qm, state, cmem, la, lc, oml, on)


def _sfinal_kernel(x_ref, onsa_ref, ohg_ref, u_ref, buf_ref, omem_ref, z_ref, pw_ref, ps_ref, wo_ref, y_ref):
    u = u_ref[...]
    lane_g = _lane_head(u.shape)
    run = u
    pooled = jnp.zeros(u.shape, F32)
    nxt = 1
    for gi, wsz in enumerate(POOL_SIZES):
        for sh in range(nxt, wsz):
            run = run + buf_ref[0, POOL_BUF - sh]
        nxt = wsz
        pooled = jnp.where(lane_g == gi, run / float(wsz), pooled)
    o_pool = _hdot(pooled - u, pw_ref[...]) * ps_ref[...]
    z = z_ref[...]
    y = jnp.concatenate([onsa_ref[...], ohg_ref[...], o_pool, omem_ref[...]], axis=1) * (z * _sigmoid(z))
    y_ref[...] = x_ref[...] + _hdot(y, wo_ref[...])


def _sfinal_call(x, onsa, ohg, u, buf, omem, z, pw, ps, wo, *, layer):
    full = lambda a: pl.BlockSpec(a.shape, lambda i: (0, 0))
    buf_spec = pl.BlockSpec((1,) + buf.shape[1:], lambda i: (layer, 0, 0, 0))
    return pl.pallas_call(
        _sfinal_kernel,
        grid=(1,),
        in_specs=[full(x), full(onsa), full(ohg), full(u), buf_spec, full(omem), full(z), full(pw), full(ps), full(wo)],
        out_specs=full(x),
        out_shape=jax.ShapeDtypeStruct(x.shape, F32),
        compiler_params=_cparams(("arbitrary",)),
    )(x, onsa, ohg, u, buf, omem, z, pw, ps, wo)


def _block_diag_ones(w):
    i = np.arange(w)
    return jnp.asarray((i[:, None] // HEAD_DIM) == (i[None, :] // HEAD_DIM), BF16)


def _rope_tables(pos):
    half = HEAD_DIM // 2
    inv = ROPE_THETA ** (-jnp.arange(half, dtype=jnp.float32) / half)
    ang = pos.astype(jnp.float32)[:, None] * inv[None, :]
    cos = jnp.cos(ang)
    sin = jnp.sin(ang)
    cos64 = jnp.concatenate([cos, cos], axis=1)
    sin64 = jnp.concatenate([-sin, sin], axis=1)
    one = jnp.ones_like(cos64)
    zero = jnp.zeros_like(sin64)
    cq = jnp.tile(cos64, (1, N_HEADS))
    sq = jnp.tile(sin64, (1, N_HEADS))
    ckv = jnp.concatenate([cos64, one] * 3, axis=1)
    skv = jnp.concatenate([sin64, zero] * 3, axis=1)
    return cq, sq, ckv, skv


def _layer_params(l, norm_g, w_in, w_out, nsa_qn, nsa_kn, cmp_pe, cmp_w1, cmp_w2, lbs, hg_on, pool_w, pool_scale,
                  mem_norm, w_mem_kv, mem_qn, mem_kn):
    w = w_in[l]
    w_pad = jnp.concatenate([w[:, :652], jnp.zeros((D_MODEL, 116), F32), w[:, 652:]], axis=1)
    ones64 = jnp.ones((HEAD_DIM,), F32)
    kn = jnp.concatenate([nsa_kn[l, 0], ones64, nsa_kn[l, 1], ones64, nsa_kn[l, 2], ones64])[None, :]
    w1 = cmp_w1[l].reshape(2, L_CMP, HEAD_DIM, CMP_HID)
    w1e = jnp.zeros((L_CMP, 4, HEAD_DIM, 2 * CMP_HID), F32)
    w1e = w1e.at[:, 0, :, :CMP_HID].set(w1[0]).at[:, 1, :, CMP_HID:].set(w1[1])
    w2e = jnp.zeros((2 * CMP_HID, 128), F32)
    w2e = w2e.at[:CMP_HID, :HEAD_DIM].set(cmp_w2[l, 0]).at[CMP_HID:, HEAD_DIM:].set(cmp_w2[l, 1])
    pe = jnp.zeros((L_CMP, 4, HEAD_DIM), F32).at[:, 0].set(cmp_pe[l, 0]).at[:, 1].set(cmp_pe[l, 1])
    wr = jnp.zeros((L_CMP, 2, HEAD_DIM, 2 * CMP_HID), F32)
    wr = wr.at[:, 0, :, :CMP_HID].set(w1[0]).at[:, 1, :, CMP_HID:].set(w1[1])
    pw = jnp.zeros((GROUP_W, GROUP_W), F32)
    for g in range(len(POOL_SIZES)):
        pw = pw.at[g * 64:(g + 1) * 64, g * 64:(g + 1) * 64].set(pool_w[l, g])
    lb = lbs[l][None, :]
    return dict(
        g=norm_g[l][None, :], w32=w_pad, w16=w_pad.astype(BF16), wo32=w_out[l], wo16=w_out[l].astype(BF16),
        qn=jnp.tile(nsa_qn[l], N_HEADS)[None, :], kn=kn, mqn=jnp.tile(mem_qn[l], N_HEADS)[None, :],
        pe=pe.reshape(1, L_CMP * 256), w1e=w1e.reshape(L_CMP * 256, 2 * CMP_HID).astype(BF16), w2e=w2e.astype(BF16),
        pe2=jnp.tile(jnp.concatenate([cmp_pe[l, 0].T, cmp_pe[l, 1].T], axis=0), (1, 2 * PAGE_SIZE // L_CMP)),
        wr2=wr.reshape(L_CMP // 2, 256, 2 * CMP_HID).astype(BF16),
        la=jnp.log(lb), lc=jnp.log1p(-lb), oml=1.0 - lb, on=hg_on[l][None, :],
        pw32=pw, pw16=pw.astype(BF16), ps=pool_scale[l][None, :],
        mg=mem_norm[l][None, :], wm16=w_mem_kv[l].astype(BF16), mkn=jnp.tile(mem_kn[l], N_HEADS)[None, :],
    )


def _prompt_layer(x, mem, p, tabs, consts):
    b, t, _ = x.shape
    bd256, bd384, mk, _ = consts
    outs = _proj_call(x.reshape(b * t, D_MODEL), p['g'], p['w16'], *tabs, p['qn'], p['kn'], p['mqn'], bd256, bd384,
                      tm=512, t_tiles=t // 512, precise=False)
    q, nsa, win, gates, hq, hf, hi, u, qm, z = [o.reshape(b, t, -1) for o in outs]
    n_blk = t // L_CMP
    cc = _cmp_call(nsa.reshape(b * n_blk, L_CMP * 256), p['pe'], p['w1e'], p['w2e'],
                   tr=min(256, b * n_blk), n_rows=b * n_blk, row_off=0)
    cc = jnp.pad(cc.reshape(b, n_blk, 128), ((0, 0), (0, 128 - n_blk), (0, 0)))
    o_nsa = _nsa_call(q, nsa, win, gates, cc, tq=NSA_TQ)
    o_hg, st = _hgrn_call(hq, hf, hi, p['la'], p['lc'], p['oml'], p['on'], mk, bd256)
    mkv = _memkv_call(mem, p['mg'], p['wm16'], p['mkn'], bd256)
    y = _final_call(x, o_nsa, o_hg, u, qm, z, mkv, p['pw16'], p['ps'], p['wo16'], tm=256)
    st5 = st.reshape(b, N_HEADS, HEAD_DIM, N_HEADS, HEAD_DIM)
    s_new = jnp.stack([st5[:, h, :, h, :] for h in range(N_HEADS)], axis=1)
    s_new = jnp.swapaxes(s_new, 2, 3)
    wb = min(WINDOW, t)
    return (y, nsa.reshape(b, t, 4, 1, HEAD_DIM), win[:, t - wb:].reshape(b, wb, 2, 1, HEAD_DIM), s_new,
            u[:, t - POOL_BUF:], mkv.reshape(b, N_MEM, 2, N_HEADS, HEAD_DIM))


def _sample_layer(l, x, p, tabs, consts, cache_pages, cache_pairs, win_cache, state, pool_buf, cmem, page_table,
                  n_phys):
    nb = x.shape[0]
    bd256, bd384, _, perm = consts
    outs = _proj_call(x, p['g'], p['w32'], *tabs, p['qn'], p['kn'], p['mqn'], bd256, bd384,
                      tm=nb, t_tiles=1, precise=True)
    q, nsa, win, gates, hq, hf, hi, u, qm, z = outs
    cc = _cmp_pages_call(cache_pages, p['pe2'], perm, p['wr2'], p['w2e'], pp=64, n_pages=n_phys,
                         page_off=l * n_phys)
    idx, o_c = _scmp_call(page_table, q, cc.reshape(n_phys, 512))
    r3 = lambda a: a.reshape(nb, 1, -1)
    o_nsa = _ssel_call(page_table, idx, cache_pairs, q, nsa, win_cache, win, gates, o_c, layer=l, n_phys=n_phys)
    o_hg, o_mem, s_new = _srest_call(r3(hq), r3(hf), r3(hi), r3(qm), state, cmem,
                                     p['la'], p['lc'], p['oml'], p['on'], layer=l)
    y = _sfinal_call(x, o_nsa.reshape(nb, 256), o_hg.reshape(nb, 256), u, pool_buf, o_mem.reshape(nb, 256), z,
                     p['pw32'], p['ps'], p['wo32'], layer=l)
    return y, nsa, win, s_new.reshape(nb, N_HEADS, HEAD_DIM, HEAD_DIM), u


def kernel(x_prompt, x_sample, mem_prompt, cache_nsa, cache_nsa_win, state_hgrn, state_pool, cache_mem,
           page_table, norm_g, w_in, w_out, nsa_qn, nsa_kn, cmp_pe, cmp_w1, cmp_w2, hg_lb, hg_on,
           pool_w, pool_scale, mem_norm, w_mem_kv, mem_qn, mem_kn):
    depth = w_in.shape[0]
    b, t, _ = x_prompt.shape
    nb = x_sample.shape[0]
    past_len = page_table.shape[1] * PAGE_SIZE
    n_phys = cache_nsa.shape[1]
    wb = cache_nsa_win.shape[2]

    lbs = jnp.cumsum(jax.nn.softmax(hg_lb.astype(jnp.float32), axis=0), axis=0)
    lbs = lbs - lbs[0:1]
    consts = (_block_diag_ones(256), _block_diag_ones(384), jnp.asarray(_hgrn_consts(), F32), _pair_perm())
    tabs_p = _rope_tables(jnp.arange(t))
    tabs_s = tuple(jnp.broadcast_to(a, (nb, a.shape[1])) for a in _rope_tables(past_len + jnp.arange(1)))

    cache_t = jnp.transpose(cache_nsa, (0, 1, 3, 4, 5, 2))
    cache_pages = cache_t.reshape(depth * n_phys, 4 * HEAD_DIM, PAGE_SIZE)
    cache_pairs = cache_t.reshape(depth * n_phys * 2, 2 * HEAD_DIM, PAGE_SIZE)
    win_cache = jnp.transpose(cache_nsa_win, (0, 1, 3, 4, 5, 2)).reshape(depth * nb, 2 * HEAD_DIM, wb)
    state = state_hgrn.reshape(depth * nb, GROUP_W, HEAD_DIM)
    pool_buf = jnp.transpose(state_pool, (0, 2, 1, 3))
    cmem = jnp.transpose(cache_mem, (0, 1, 3, 4, 5, 2)).reshape(depth * nb, 2 * GROUP_W, N_MEM)

    xp, xs = x_prompt, x_sample.reshape(nb, D_MODEL)
    acc = [[] for _ in range(9)]
    for l in range(depth):
        p = _layer_params(l, norm_g, w_in, w_out, nsa_qn, nsa_kn, cmp_pe, cmp_w1, cmp_w2, lbs, hg_on, pool_w,
                          pool_scale, mem_norm, w_mem_kv, mem_qn, mem_kn)
        xp, a, bw, c, d, e = _prompt_layer(xp, mem_prompt, p, tabs_p, consts)
        xs, sa, sw, sc, sd = _sample_layer(l, xs, p, tabs_s, consts, cache_pages, cache_pairs, win_cache, state,
                                           pool_buf, cmem, page_table, n_phys)
        new_win_s = jnp.concatenate([cache_nsa_win[l][:, 1:], sw.reshape(nb, 1, 2, 1, HEAD_DIM)], axis=1)[:, -wb:]
        new_pool_s = jnp.concatenate([state_pool[l][:, 1:], sd.reshape(nb, 1, GROUP_W)], axis=1)
        for lst, val in zip(acc, (a, bw, c, d, e, sa.reshape(nb, 1, 4, 1, HEAD_DIM), new_win_s, sc, new_pool_s)):
            lst.append(val)
    return (xp, xs.reshape(nb, 1, D_MODEL)) + tuple(jnp.stack(v) for v in acc)
```

```python
import functools

import numpy as np
import jax
import jax.numpy as jnp
from jax import lax
from jax.experimental import pallas as pl
from jax.experimental.pallas import tpu as pltpu

D_MODEL = 1024
HEAD_DIM = 64
GROUP_W = 256
N_HEADS = 4
L_CMP = 32
L_SLC = 64
N_SEL = 16
WINDOW = 512
CMP_HID = 256
POOL_SIZES = (2, 4, 8, 16)
POOL_BUF = 15
N_MEM = 256
PAGE_SIZE = 128
ROPE_THETA = 10000.0
EPS = 1e-6
QK_SCALE = HEAD_DIM ** -0.5

PROJ_W = 3072
HG_CHUNK = 256
NSA_TQ = 256
NEG = -1e30
VMEM_LIMIT = 56 * 1024 * 1024

F32 = jnp.float32
BF16 = jnp.bfloat16
HI = lax.Precision.HIGHEST


def _cparams(sem):
    return pltpu.CompilerParams(dimension_semantics=sem, vmem_limit_bytes=VMEM_LIMIT)


def _bdot(a, b):
    return jnp.dot(a.astype(BF16), b.astype(BF16), preferred_element_type=F32)


def _bdot_nt(a, b):
    return lax.dot_general(a.astype(BF16), b.astype(BF16), (((1,), (1,)), ((), ())), preferred_element_type=F32)


def _hdot(a, b):
    return jnp.dot(a, b, precision=HI, preferred_element_type=F32)


def _hdot_nt(a, b):
    return lax.dot_general(a, b, (((1,), (1,)), ((), ())), precision=HI, preferred_element_type=F32)


def _dot2(a, b16):
    hi = a.astype(BF16)
    lo = (a - hi.astype(F32)).astype(BF16)
    return jnp.dot(hi, b16, preferred_element_type=F32) + jnp.dot(lo, b16, preferred_element_type=F32)


def _dot2_left(a16, b):
    hi = b.astype(BF16)
    lo = (b - hi.astype(F32)).astype(BF16)
    return jnp.dot(a16, hi, preferred_element_type=F32) + jnp.dot(a16, lo, preferred_element_type=F32)


def _seg_rms(x, bd16, gain):
    ms = _dot2(x * x, bd16) * (1.0 / HEAD_DIM)
    return x * lax.rsqrt(ms + EPS) * gain


def _rope(x, cos, sin_signed):
    w = x.shape[1]
    lane = lax.broadcasted_iota(jnp.int32, x.shape, 1)
    first = (lane & 32) == 0
    rot = jnp.where(first, pltpu.roll(x, w - 32, 1), pltpu.roll(x, 32, 1))
    return x * cos + rot * sin_signed


def _sigmoid(x):
    return jax.nn.sigmoid(x)


def _lane_head(shape):
    return lax.broadcasted_iota(jnp.int32, shape, 1) // HEAD_DIM


def _proj_kernel(x_ref, g_ref, w_ref, cq_ref, sq_ref, ckv_ref, skv_ref, qn_ref, kn_ref, mqn_ref, bd256_ref, bd384_ref,
                 q_out, nsa_out, win_out, gate_out, hq_out, hf_out, hi_out, u_out, qm_out, z_out, *, precise):
    x = x_ref[...]
    ms = jnp.mean(x * x, axis=-1, keepdims=True)
    h = x * lax.rsqrt(ms + EPS) * g_ref[...]
    proj = _hdot(h, w_ref[...]) if precise else _bdot(h, w_ref[...])
    bd256 = bd256_ref[...]
    q = _rope(_seg_rms(proj[:, 0:256], bd256, qn_ref[...]), cq_ref[...], sq_ref[...])
    q_out[...] = q
    kv = proj[:, 256:640]
    kvn = _rope(_seg_rms(kv, bd384_ref[...], kn_ref[...]), ckv_ref[...], skv_ref[...])
    lane = lax.broadcasted_iota(jnp.int32, kv.shape, 1)
    is_key = ((lane // HEAD_DIM) & 1) == 0
    kv = jnp.where(is_key, kvn, kv)
    nsa_out[...] = kv[:, 0:256]
    win_out[...] = kv[:, 256:384]
    gate_out[...] = _sigmoid(proj[:, 640:768])
    hq_out[...] = proj[:, 768:1024]
    hf_out[...] = proj[:, 1024:1280]
    hi_out[...] = proj[:, 1280:1536]
    u_out[...] = proj[:, 1536:1792]
    qm_out[...] = _seg_rms(proj[:, 1792:2048], bd256, mqn_ref[...])
    z_out[...] = proj[:, 2048:3072]


def _proj_call(x, g, w, cq, sq, ckv, skv, qn, kn, mqn, bd256, bd384, *, tm, t_tiles, precise):
    n = x.shape[0]
    row = lambda wd: pl.BlockSpec((tm, wd), lambda i: (i, 0))
    tab = lambda wd: pl.BlockSpec((tm, wd), lambda i: (i % t_tiles, 0))
    full = lambda a: pl.BlockSpec(a.shape, lambda i: (0, 0))
    widths = (256, 256, 128, 128, 256, 256, 256, 256, 256, 1024)
    return pl.pallas_call(
        functools.partial(_proj_kernel, precise=precise),
        grid=(n // tm,),
        in_specs=[row(D_MODEL), full(g), full(w), tab(256), tab(256), tab(384), tab(384),
                  full(qn), full(kn), full(mqn), full(bd256), full(bd384)],
        out_specs=[row(wd) for wd in widths],
        out_shape=[jax.ShapeDtypeStruct((n, wd), F32) for wd in widths],
        compiler_params=_cparams(("parallel",)),
    )(x, g, w, cq, sq, ckv, skv, qn, kn, mqn, bd256, bd384)


def _cmp_kernel(x_ref, pe_ref, w1_ref, w2_ref, o_ref):
    x = x_ref[...] + pe_ref[...]
    h = _bdot(x, w1_ref[...])
    h = h * _sigmoid(h)
    o_ref[...] = _bdot(h, w2_ref[...])


def _cmp_call(x, pe, w1, w2, *, tr, n_rows, row_off):
    off = row_off // tr
    full = lambda a: pl.BlockSpec(a.shape, lambda i: (0, 0))
    return pl.pallas_call(
        _cmp_kernel,
        grid=(n_rows // tr,),
        in_specs=[pl.BlockSpec((tr, x.shape[1]), lambda i: (i + off, 0)), full(pe), full(w1), full(w2)],
        out_specs=pl.BlockSpec((tr, 128), lambda i: (i, 0)),
        out_shape=jax.ShapeDtypeStruct((n_rows, 128), F32),
        compiler_params=_cparams(("parallel",)),
    )(x, pe, w1, w2)


def _cmp_pages_kernel(pt_ref, pages_hbm, pe_ref, perm_ref, wr_ref, w2_ref, o_ref, pg_buf, xs_scr, sem,
                      *, pp, nb, npg, pg_off):
    per_seq = npg // pp
    n_out = pp * (PAGE_SIZE // L_CMP)
    pe2 = pe_ref[...]
    perm = perm_ref[...]

    def copies(step, slot):
        b = step // per_seq
        i0 = (step - b * per_seq) * pp
        return [pltpu.make_async_copy(pages_hbm.at[pg_off + pt_ref[b, i0 + j], pl.ds(0, 2 * HEAD_DIM), :],
                                      pg_buf.at[slot, j], sem.at[slot]) for j in range(pp)]

    for cp in copies(0, 0):
        cp.start()

    def trip(step, carry):
        slot = step % 2

        @pl.when(step + 1 < nb * per_seq)
        def _():
            for cp in copies(step + 1, 1 - slot):
                cp.start()

        for cp in copies(step, slot):
            cp.wait()
        for i in range(pp // 2):
            m2 = jnp.concatenate([pg_buf[slot, 2 * i], pg_buf[slot, 2 * i + 1]], axis=1) + pe2
            xp = lax.dot_general(perm, m2.astype(BF16), (((1,), (1,)), ((), ())), preferred_element_type=F32)
            xs_scr[:, 8 * i:8 * i + 8, :] = xp.reshape(L_CMP, 8, 128)
        acc = jnp.zeros((n_out, 2 * CMP_HID), F32)
        for r2 in range(L_CMP // 2):
            slab = jnp.concatenate([xs_scr[2 * r2], xs_scr[2 * r2 + 1]], axis=1)
            acc = acc + _bdot(slab, wr_ref[r2])
        h = acc * _sigmoid(acc)
        o_ref[pl.ds(pl.multiple_of(step * n_out, n_out), n_out), :] = _bdot(h, w2_ref[...])
        return carry

    lax.fori_loop(0, nb * per_seq, trip, 0)


def _cmp_pages_call(page_table, pages, pe2, perm, wr2, w2, *, pp, page_off):
    nb, npg = page_table.shape
    n_out = pp * (PAGE_SIZE // L_CMP)
    full = lambda a: pl.BlockSpec(a.shape, lambda i, pt: (0,) * a.ndim)
    n_rows = nb * npg * (PAGE_SIZE // L_CMP)
    return pl.pallas_call(
        functools.partial(_cmp_pages_kernel, pp=pp, nb=nb, npg=npg, pg_off=page_off),
        grid_spec=pltpu.PrefetchScalarGridSpec(
            num_scalar_prefetch=1,
            grid=(1,),
            in_specs=[pl.BlockSpec(memory_space=pl.ANY), full(pe2), full(perm), full(wr2), full(w2)],
            out_specs=pl.BlockSpec((n_rows, 128), lambda i, pt: (0, 0)),
            scratch_shapes=[pltpu.VMEM((2, pp, 2 * HEAD_DIM, PAGE_SIZE), F32), pltpu.VMEM((L_CMP, n_out, 128), F32),
                            pltpu.SemaphoreType.DMA((2,))],
        ),
        out_shape=jax.ShapeDtypeStruct((n_rows, 128), F32),
        compiler_params=_cparams(("arbitrary",)),
    )(page_table, pages, pe2, perm, wr2, w2)


def _pair_perm():
    a = np.zeros((L_CMP, 8, 2, PAGE_SIZE), np.float32)
    for r in range(L_CMP):
        for pq in range(2):
            for nb in range(PAGE_SIZE // L_CMP):
                a[r, 4 * pq + nb, pq, L_CMP * nb + r] = 1.0
    return jnp.asarray(a.reshape(L_CMP * 8, 2 * PAGE_SIZE), BF16)


def _topk_mask(score, k, idx_f, axis=1):
    sel = jnp.zeros(score.shape, jnp.bool_)
    s = score
    big = jnp.float32(1e9)
    picks = []
    for _ in range(k):
        m = jnp.max(s, axis=axis, keepdims=True)
        cand = jnp.where(s == m, idx_f, big)
        pick = jnp.min(cand, axis=axis, keepdims=True)
        hit = idx_f == pick
        sel = sel | hit
        s = jnp.where(hit, -jnp.inf, s)
        picks.append(pick)
    return sel, picks


def _flash_step(carry, k16, vt16, bias, qs_scr):
    return _flash_update(carry, _scores(k16, qs_scr), vt16, bias)


def _scores(k16, qs_scr):
    return lax.dot_general(k16, qs_scr[...], (((1,), (1,)), ((), ())), preferred_element_type=F32)


def _flash_update(carry, s, vt16, bias):
    m, l, acc = carry
    if bias is not None:
        s = s + (jnp.concatenate([bias] * N_HEADS, axis=1) if bias.ndim == 2 else bias)
    m_new = jnp.maximum(m, jnp.max(s, axis=0, keepdims=True))
    alpha = jnp.exp(m - m_new)
    p = jnp.exp(s - m_new)
    l = alpha * l + jnp.sum(p, axis=0, keepdims=True)
    acc = alpha * acc + jnp.dot(vt16, p.astype(BF16), preferred_element_type=F32)
    return m_new, l, acc


def _nsa_kernel(q_ref, nsa_ref, win_ref, gate_ref, cc_ref, ex_ref, o_ref, qs_scr, ks_scr, kw_scr, vt_scr, wt_scr,
                *, tq):
    i = pl.program_id(1)
    t_len = nsa_ref.shape[1]
    assert WINDOW == 2 * tq and t_len % tq == 0
    n_cmp = HEAD_DIM

    @pl.when(i == 0)
    def _():
        for c in range(t_len // tq):
            sl = slice(c * tq, (c + 1) * tq)
            ks_scr[c] = jnp.concatenate([nsa_ref[0, sl, 128:192].astype(BF16), ex_ref[sl, :]], axis=1)
            kw_scr[c] = jnp.concatenate([win_ref[0, sl, 0:HEAD_DIM].astype(BF16),
                                         jnp.zeros((tq, n_cmp), BF16)], axis=1)
            vt_scr[c] = nsa_ref[0, sl, 128:256].T.astype(BF16)
            wt_scr[c] = win_ref[0, sl, :].T.astype(BF16)

    t0 = i * tq
    r4 = N_HEADS * tq
    q = q_ref[0] * QK_SCALE
    for h in range(N_HEADS):
        qs_scr[h * tq:(h + 1) * tq, 0:HEAD_DIM] = q[:, h * HEAD_DIM:(h + 1) * HEAD_DIM].astype(BF16)
    qs_scr[:, HEAD_DIM:2 * HEAD_DIM] = jnp.zeros((r4, n_cmp), BF16)
    t_pos = t0 + lax.broadcasted_iota(jnp.int32, (1, tq), 1)
    t_pos4 = t0 + lax.broadcasted_iota(jnp.int32, (1, r4), 1) % tq

    cc = cc_ref[0]
    kcc16 = jnp.concatenate([cc[0:n_cmp, 0:HEAD_DIM], jnp.zeros((n_cmp, n_cmp), F32)], axis=1).astype(BF16)
    vcct16 = cc.T[HEAD_DIM:2 * HEAD_DIM, 0:n_cmp].astype(BF16)
    done = (lax.broadcasted_iota(jnp.int32, (n_cmp, r4), 0) + 1) * L_CMP - 1 <= t_pos4
    s = lax.dot_general(kcc16, qs_scr[...], (((1,), (1,)), ((), ())), preferred_element_type=F32)
    s = jnp.where(done, s, NEG)
    m = jnp.max(s, axis=0, keepdims=True)
    m = jnp.where(m < 0.5 * NEG, 0.0, m)
    p = jnp.where(done, jnp.exp(s - m), 0.0)
    p = p / jnp.maximum(jnp.sum(p, axis=0, keepdims=True), 1e-30)
    o_c = jnp.dot(vcct16, p.astype(BF16), preferred_element_type=F32)
    imp = p[:, 0:tq] + p[:, tq:2 * tq] + p[:, 2 * tq:3 * tq] + p[:, 3 * tq:4 * tq]

    n_idx = lax.broadcasted_iota(jnp.int32, (n_cmp, tq), 0)
    imp2 = imp + pltpu.roll(imp, n_cmp - 1, 0)
    blk_t = t_pos // L_SLC
    j = n_idx // 2
    valid = ((n_idx & 1) == 0) & (j < t_len // L_SLC)
    avail = valid & (j <= blk_t)
    forced = (j == 0) | (j == blk_t) | (j == blk_t - 1)
    score = jnp.where(avail, jnp.where(forced, jnp.inf, imp2), -jnp.inf)
    sel, _ = _topk_mask(score, N_SEL, n_idx.astype(F32), axis=0)
    sel_f = jnp.where(sel & avail, 0.0, NEG)
    sel_t = jnp.concatenate([sel_f, jnp.zeros((128 - n_cmp, tq), F32)], axis=0).T[:, 0:n_cmp].astype(BF16)
    for h in range(N_HEADS):
        qs_scr[h * tq:(h + 1) * tq, HEAD_DIM:2 * HEAD_DIM] = sel_t

    key_l = lax.broadcasted_iota(jnp.int32, (tq, tq), 0)
    qry_l = lax.broadcasted_iota(jnp.int32, (tq, tq), 1)
    causal = jnp.where(key_l <= qry_l, 0.0, NEG)
    tail = jnp.where(key_l > qry_l, 0.0, NEG)
    init = (jnp.full((1, r4), NEG, F32), jnp.zeros((1, r4), F32), jnp.zeros((HEAD_DIM, r4), F32))

    def sel_body(c, carry):
        return _flash_step(carry, ks_scr[c], vt_scr[c, HEAD_DIM:2 * HEAD_DIM, :], None, qs_scr)

    st_s = lax.fori_loop(0, i, sel_body, init)
    st_s = _flash_step(st_s, ks_scr[i], vt_scr[i, HEAD_DIM:2 * HEAD_DIM, :], causal, qs_scr)

    c1 = jnp.maximum(i - 1, 0)
    c2 = jnp.maximum(i - 2, 0)
    st_w = _flash_step(init, kw_scr[i], wt_scr[i, HEAD_DIM:2 * HEAD_DIM, :], causal, qs_scr)
    st_w = _flash_step(st_w, kw_scr[c1], wt_scr[c1, HEAD_DIM:2 * HEAD_DIM, :], jnp.where(i >= 1, 0.0, NEG), qs_scr)
    st_w = _flash_step(st_w, kw_scr[c2], wt_scr[c2, HEAD_DIM:2 * HEAD_DIM, :],
                       tail + jnp.where(i >= 2, 0.0, NEG), qs_scr)

    gt = gate_ref[0].T
    g_c, g_s, g_w = (jnp.concatenate([gt[3 * h + br:3 * h + br + 1] for h in range(N_HEADS)], axis=1)
                     for br in range(3))
    o_s = st_s[2] / jnp.maximum(st_s[1], 1e-30)
    o_w = st_w[2] / jnp.maximum(st_w[1], 1e-30)
    o = g_c * o_c + g_s * o_s + g_w * o_w
    o_ref[0] = jnp.concatenate([o[:, h * tq:(h + 1) * tq] for h in range(N_HEADS)], axis=0).T


def _nsa_call(q, nsa, win, gates, cc, *, tq):
    b, t, _ = q.shape
    tok = np.arange(t)[:, None]
    onehot = jnp.asarray(np.arange(HEAD_DIM)[None, :] == 2 * (tok // L_SLC), BF16)
    n_ch = t // tq
    return pl.pallas_call(
        functools.partial(_nsa_kernel, tq=tq),
        grid=(b, n_ch),
        in_specs=[pl.BlockSpec((1, tq, 256), lambda bi, i: (bi, i, 0)),
                  pl.BlockSpec((1, t, 256), lambda bi, i: (bi, 0, 0)),
                  pl.BlockSpec((1, t, 128), lambda bi, i: (bi, 0, 0)),
                  pl.BlockSpec((1, tq, 128), lambda bi, i: (bi, i, 0)),
                  pl.BlockSpec((1, 128, 128), lambda bi, i: (bi, 0, 0)),
                  pl.BlockSpec((t, HEAD_DIM), lambda bi, i: (0, 0))],
        out_specs=pl.BlockSpec((1, tq, 256), lambda bi, i: (bi, i, 0)),
        out_shape=jax.ShapeDtypeStruct((b, t, 256), F32),
        scratch_shapes=[pltpu.VMEM((N_HEADS * tq, 128), BF16), pltpu.VMEM((n_ch, tq, 128), BF16),
                        pltpu.VMEM((n_ch, tq, 128), BF16), pltpu.VMEM((n_ch, 128, tq), BF16),
                        pltpu.VMEM((n_ch, 128, tq), BF16)],
        compiler_params=_cparams(("parallel", "arbitrary")),
    )(q, nsa, win, gates, cc, onehot)


def _hgrn_consts(c=HG_CHUNK):
    t = np.arange(c)[:, None]
    k = np.arange(c)[None, :]
    masks = []
    m = 1
    while m < c:
        upper = (t % (2 * m)) >= m
        masks.append(upper & ((k // (2 * m)) == t // (2 * m)) & ((k % (2 * m)) < m))
        m *= 2
    masks.append(t == k)
    return np.stack(masks).astype(np.float32)


def _block_row_bcast(p, m):
    c, w = p.shape
    if 2 * m >= 8:
        p3 = p.reshape(c // (2 * m), 2 * m, w)
        return jnp.broadcast_to(p3[:, m - 1:m, :], p3.shape).reshape(c, w)
    off = lax.broadcasted_iota(jnp.int32, p.shape, 0) % (2 * m)
    out = p
    for o in range(2 * m):
        if o != m - 1:
            out = jnp.where(off == o, pltpu.roll(p, (o - (m - 1)) % c, 0), out)
    return out


def _log_decay(fr, la, lc):
    ls = jnp.minimum(fr, 0.0) - jnp.log1p(jnp.exp(-jnp.abs(fr)))
    b = lc + ls
    return jnp.maximum(la, b) + jnp.log1p(jnp.exp(-jnp.abs(la - b)))


def _hgrn_kernel(hq_ref, hf_ref, hi_ref, la_ref, lc_ref, oml_ref, on_ref, mk_ref, bd_ref,
                 o_ref, st_ref, st_scr):
    c = HG_CHUNK
    jt = pl.program_id(1)

    @pl.when(jt == 0)
    def _():
        st_scr[...] = jnp.zeros_like(st_scr)

    q = hq_ref[0]
    fr = hf_ref[0]
    v = hi_ref[0]
    logf = _log_decay(fr, la_ref[...], lc_ref[...])
    k = oml_ref[...] * _sigmoid(-fr)
    row = lax.broadcasted_iota(jnp.int32, (c, GROUP_W), 0)
    pre = logf
    decays = []
    m = 1
    while m < c:
        mid = _block_row_bcast(pre, m)
        upper = (row % (2 * m)) >= m
        decays.append(jnp.exp(jnp.where(upper, pre, mid - pre)))
        pre = pre + jnp.where(upper, mid, 0.0)
        m *= 2
    gc = pre
    gl = gc[c - 1:c, :]
    lane_h = _lane_head((c, GROUP_W))
    n_lvl = mk_ref.shape[0]
    a_tot = [jnp.zeros((c, c), F32) for _ in range(N_HEADS)]
    for lv in range(n_lvl):
        qt = q * decays[lv] if lv < len(decays) else q
        kt = (k * decays[lv] if lv < len(decays) else k).astype(BF16)
        qs = jnp.concatenate([jnp.where(lane_h == h, qt, 0.0).astype(BF16) for h in range(N_HEADS)], axis=0)
        a_l = lax.dot_general(qs, kt, (((1,), (1,)), ((), ())), preferred_element_type=F32)
        mk = mk_ref[lv]
        for h in range(N_HEADS):
            a_tot[h] = a_tot[h] + mk * a_l[h * c:(h + 1) * c]
    a_all = jnp.concatenate(a_tot, axis=0)
    r = _bdot(a_all, v)
    o = jnp.zeros((c, GROUP_W), F32)
    for h in range(N_HEADS):
        o = o + jnp.where(lane_h == h, r[h * c:(h + 1) * c], 0.0)
    st = st_scr[...]
    o = o + _bdot_nt(q * jnp.exp(gc), st)
    khat = k * jnp.exp(gl - gc)
    upd = jnp.dot(v.T.astype(BF16), khat.astype(BF16), preferred_element_type=F32)
    row_h = lax.broadcasted_iota(jnp.int32, (GROUP_W, GROUP_W), 0) // HEAD_DIM
    col_h = _lane_head((GROUP_W, GROUP_W))
    st_new = jnp.exp(gl) * st + jnp.where(row_h == col_h, upd, 0.0)
    st_scr[...] = st_new
    st_ref[0] = st_new
    o_ref[0] = _seg_rms(o, bd_ref[...], on_ref[...])


def _hgrn_call(hq, hf, hi, la, lc, oml, on, mk, bd256):
    b, t, _ = hq.shape
    c = HG_CHUNK
    row = pl.BlockSpec((1, c, GROUP_W), lambda bi, j: (bi, j, 0))
    full2 = lambda a: pl.BlockSpec(a.shape, lambda bi, j: (0, 0))
    return pl.pallas_call(
        _hgrn_kernel,
        grid=(b, t // c),
        in_specs=[row, row, row, full2(la), full2(lc), full2(oml), full2(on),
                  pl.BlockSpec(mk.shape, lambda bi, j: (0, 0, 0)), full2(bd256)],
        out_specs=[row, pl.BlockSpec((1, GROUP_W, GROUP_W), lambda bi, j: (bi, 0, 0))],
        out_shape=[jax.ShapeDtypeStruct((b, t, GROUP_W), F32), jax.ShapeDtypeStruct((b, GROUP_W, GROUP_W), F32)],
        scratch_shapes=[pltpu.VMEM((GROUP_W, GROUP_W), F32)],
        compiler_params=_cparams(("parallel", "arbitrary")),
    )(hq, hf, hi, la, lc, oml, on, mk, bd256)


def _memkv_kernel(m_ref, g_ref, w_ref, kn_ref, bd_ref, o_ref):
    x = m_ref[0]
    ms = jnp.mean(x * x, axis=-1, keepdims=True)
    h = x * lax.rsqrt(ms + EPS) * g_ref[...]
    kv = _bdot(h, w_ref[...])
    o_ref[0, :, 0:GROUP_W] = _seg_rms(kv[:, 0:GROUP_W], bd_ref[...], kn_ref[...])
    o_ref[0, :, GROUP_W:2 * GROUP_W] = kv[:, GROUP_W:2 * GROUP_W]


def _memkv_call(mem, g, w, kn, bd256):
    b = mem.shape[0]
    full = lambda a: pl.BlockSpec(a.shape, lambda i: (0, 0))
    return pl.pallas_call(
        _memkv_kernel,
        grid=(b,),
        in_specs=[pl.BlockSpec((1, N_MEM, D_MODEL), lambda i: (i, 0, 0)), full(g), full(w), full(kn), full(bd256)],
        out_specs=pl.BlockSpec((1, N_MEM, 2 * GROUP_W), lambda i: (i, 0, 0)),
        out_shape=jax.ShapeDtypeStruct((b, N_MEM, 2 * GROUP_W), F32),
        compiler_params=_cparams(("parallel",)),
    )(mem, g, w, kn, bd256)


def _final_kernel(x_ref, onsa_ref, ohg_ref, u_ref, halo_ref, qm_ref, z_ref, mkv_ref, pw_ref, ps_ref, wo_ref,
                  y_ref, ext_scr, *, tm):
    jt = pl.program_id(1)
    u = u_ref[0]
    halo = halo_ref[0]
    ext_scr[0:16, :] = jnp.where(jt == 0, 0.0, halo)
    ext_scr[16:16 + tm, :] = u
    pos = jt * tm + lax.broadcasted_iota(jnp.int32, (tm, 1), 0)
    lane_g = _lane_head((tm, GROUP_W))
    run = u
    pooled = jnp.zeros((tm, GROUP_W), F32)
    nxt = 1
    for gi, wsz in enumerate(POOL_SIZES):
        for sh in range(nxt, wsz):
            run = run + ext_scr[16 - sh:16 - sh + tm, :]
        nxt = wsz
        cnt = jnp.minimum(wsz, pos + 1).astype(F32)
        pooled = jnp.where(lane_g == gi, run / cnt, pooled)
    o_pool = _bdot(pooled - u, pw_ref[...]) * ps_ref[...]

    qm = qm_ref[0] * QK_SCALE
    mkv = mkv_ref[0]
    heads = []
    for h in range(N_HEADS):
        sl = slice(h * HEAD_DIM, (h + 1) * HEAD_DIM)
        s = _bdot_nt(qm[:, sl], mkv[:, sl])
        s = s - jnp.max(s, axis=1, keepdims=True)
        p = jnp.exp(s)
        p = p / jnp.sum(p, axis=1, keepdims=True)
        heads.append(_bdot(p, mkv[:, GROUP_W + h * HEAD_DIM:GROUP_W + (h + 1) * HEAD_DIM]))
    o_mem = jnp.concatenate(heads, axis=1)

    z = z_ref[0]
    y = jnp.concatenate([onsa_ref[0], ohg_ref[0], o_pool, o_mem], axis=1) * (z * _sigmoid(z))
    y_ref[0] = x_ref[0] + _bdot(y, wo_ref[...])


def _final_call(x, onsa, ohg, u, qm, z, mkv, pw, ps, wo, *, tm):
    b, t, _ = x.shape
    row = lambda wd: pl.BlockSpec((1, tm, wd), lambda bi, j: (bi, j, 0))
    full2 = lambda a: pl.BlockSpec(a.shape, lambda bi, j: (0, 0))
    hb = tm // 16
    return pl.pallas_call(
        functools.partial(_final_kernel, tm=tm),
        grid=(b, t // tm),
        in_specs=[row(D_MODEL), row(GROUP_W), row(GROUP_W), row(GROUP_W),
                  pl.BlockSpec((1, 16, GROUP_W), lambda bi, j: (bi, jnp.maximum(j * hb - 1, 0), 0)),
                  row(GROUP_W), row(D_MODEL),
                  pl.BlockSpec((1, N_MEM, 2 * GROUP_W), lambda bi, j: (bi, 0, 0)),
                  full2(pw), full2(ps), full2(wo)],
        out_specs=row(D_MODEL),
        out_shape=jax.ShapeDtypeStruct((b, t, D_MODEL), F32),
        scratch_shapes=[pltpu.VMEM((tm + 16, GROUP_W), F32)],
        compiler_params=_cparams(("parallel", "arbitrary")),
    )(x, onsa, ohg, u, u, qm, z, mkv, pw, ps, wo)


def _head_rows(row):
    r8 = lax.broadcasted_iota(jnp.int32, (8, GROUP_W), 0)
    qm = jnp.where(r8 == _lane_head((8, GROUP_W)), jnp.broadcast_to(row, (8, GROUP_W)), 0.0)
    return qm[:, 0:64] + qm[:, 64:128] + qm[:, 128:192] + qm[:, 192:256]


def _scmp_kernel(q_ref, cc_ref, idx_ref, oc_ref, a_scr, b_scr, *, nb, npg):
    b = pl.program_id(0)
    gb = cc_ref[...]
    qh = _head_rows(q_ref[pl.ds(b, 1), :]) * QK_SCALE
    ss = [_hdot_nt(qh, gb[:, r * 128:r * 128 + HEAD_DIM]) for r in range(4)]
    m = ss[0].max(axis=1, keepdims=True)
    for r in range(1, 4):
        m = jnp.maximum(m, ss[r].max(axis=1, keepdims=True))
    ps = [jnp.exp(s - m) for s in ss]
    l = ps[0].sum(axis=1, keepdims=True)
    for r in range(1, 4):
        l = l + ps[r].sum(axis=1, keepdims=True)
    inv = 1.0 / jnp.maximum(l, 1e-30)
    ps = [p * inv for p in ps]
    oc = _hdot(ps[0], gb[:, HEAD_DIM:128])
    for r in range(1, 4):
        oc = oc + _hdot(ps[r], gb[:, r * 128 + HEAD_DIM:(r + 1) * 128])
    oc_ref[pl.ds(pl.multiple_of(b * 8, 8), 8), :] = oc
    hmask = lax.broadcasted_iota(jnp.int32, ss[0].shape, 0) < N_HEADS
    imps = [jnp.sum(jnp.where(hmask, p, 0.0), axis=0, keepdims=True) for p in ps]
    a_scr[pl.ds(b, 1), :] = imps[0] + imps[1]
    b_scr[pl.ds(b, 1), :] = imps[2] + imps[3]

    @pl.when(b == nb - 1)
    def _():
        score = jnp.concatenate([a_scr[...], b_scr[...]], axis=1)
        lane = lax.broadcasted_iota(jnp.int32, score.shape, 1)
        jidx = jnp.where(lane < npg, 2 * lane, 2 * (lane - npg) + 1)
        forced = (jidx == 0) | (jidx == 2 * npg - 1)
        score = jnp.where(forced, jnp.inf, score)
        _, picks = _topk_mask(score, N_SEL - 1, jidx.astype(F32))
        out_lane = lax.broadcasted_iota(jnp.int32, (nb, 128), 1)
        out = jnp.zeros((nb, 128), jnp.int32)
        for r, pk in enumerate(picks):
            out = jnp.where(out_lane == r, pk.astype(jnp.int32), out)
        idx_ref[...] = out


def _scmp_call(q, cc_rows, *, nb, npg):
    return pl.pallas_call(
        functools.partial(_scmp_kernel, nb=nb, npg=npg),
        grid=(nb,),
        in_specs=[pl.BlockSpec(q.shape, lambda b: (0, 0)), pl.BlockSpec((npg, 512), lambda b: (b, 0))],
        out_specs=[pl.BlockSpec((nb, 128), lambda b: (0, 0)), pl.BlockSpec((nb * 8, HEAD_DIM), lambda b: (0, 0))],
        out_shape=[jax.ShapeDtypeStruct((nb, 128), jnp.int32), jax.ShapeDtypeStruct((nb * 8, HEAD_DIM), F32)],
        scratch_shapes=[pltpu.VMEM((nb, 128), F32), pltpu.VMEM((nb, 128), F32)],
        compiler_params=_cparams(("arbitrary",)),
    )(q, cc_rows)


def _ssel_kernel(pt_ref, idx_ref, pairs_hbm, wc_hbm, q_ref, new_ref, wnew_ref, gate_ref, oc_ref, o_ref,
                 blk_buf, wc_buf, blk_sem, wc_sem, *, n_gather, nb, pg_off, wc_off):
    def copies(b, slot):
        out = []
        for r in range(n_gather):
            src = (pg_off + pt_ref[b, idx_ref[b, r] // 2]) * 2 + 1
            out.append(pltpu.make_async_copy(pairs_hbm.at[src], blk_buf.at[slot, r], blk_sem.at[slot]))
        out.append(pltpu.make_async_copy(wc_hbm.at[wc_off + b], wc_buf.at[slot], wc_sem.at[slot]))
        return out

    for cp in copies(0, 0):
        cp.start()
    lane_half = lax.broadcasted_iota(jnp.int32, (8, PAGE_SIZE), 1) // L_SLC

    def per_seq(b, carry):
        slot = b % 2

        @pl.when(b + 1 < nb)
        def _():
            for cp in copies(b + 1, 1 - slot):
                cp.start()

        for cp in copies(b, slot):
            cp.wait()
        qh = _head_rows(q_ref[pl.ds(b, 1), :]) * QK_SCALE
        new = new_ref[pl.ds(b, 1), :]
        s_n = jnp.sum(qh * new[:, 128:192], axis=1, keepdims=True)
        scores = []
        m = s_n
        for r in range(n_gather):
            half = idx_ref[b, r] % 2
            s = _hdot(qh, blk_buf[slot, r, 0:HEAD_DIM, :])
            s = jnp.where(lane_half == half, s, NEG)
            scores.append(s)
            m = jnp.maximum(m, s.max(axis=1, keepdims=True))
        p_n = jnp.exp(s_n - m)
        l = p_n
        acc = p_n * new[:, 192:256]
        for r in range(n_gather):
            p = jnp.exp(scores[r] - m)
            l = l + p.sum(axis=1, keepdims=True)
            acc = acc + _hdot_nt(p, blk_buf[slot, r, HEAD_DIM:2 * HEAD_DIM, :])
        o_s = acc / jnp.maximum(l, 1e-30)
        wnew = wnew_ref[pl.ds(b, 1), :]
        s_w = _hdot(qh, wc_buf[slot, 0:HEAD_DIM, :])
        keep = lax.broadcasted_iota(jnp.int32, s_w.shape, 1) >= s_w.shape[1] + 1 - WINDOW
        s_w = jnp.where(keep, s_w, NEG)
        s_wn = jnp.sum(qh * wnew[:, 0:HEAD_DIM], axis=1, keepdims=True)
        m_w = jnp.maximum(s_w.max(axis=1, keepdims=True), s_wn)
        p_w = jnp.where(keep, jnp.exp(s_w - m_w), 0.0)
        p_wn = jnp.exp(s_wn - m_w)
        l_w = p_w.sum(axis=1, keepdims=True) + p_wn
        o_w = (_hdot_nt(p_w, wc_buf[slot, HEAD_DIM:128, :]) + p_wn * wnew[:, HEAD_DIM:128]) / jnp.maximum(l_w, 1e-30)
        o_c = oc_ref[pl.ds(pl.multiple_of(b * 8, 8), 8), :]
        g = gate_ref[pl.ds(b, 1), :]
        outs = []
        for h in range(N_HEADS):
            outs.append(g[:, 3 * h:3 * h + 1] * o_c[h:h + 1] + g[:, 3 * h + 1:3 * h + 2] * o_s[h:h + 1]
                        + g[:, 3 * h + 2:3 * h + 3] * o_w[h:h + 1])
        o_ref[pl.ds(b, 1), :] = jnp.concatenate(outs, axis=1)
        return carry

    lax.fori_loop(0, nb, per_seq, 0)


def _ssel_call(page_table, idx, cache_pairs, q, new_rows, win_cache, win_new, gates, o_c, *, layer, n_phys):
    nb = q.shape[0]
    n_gather = N_SEL - 1
    wb = win_cache.shape[2]
    full = lambda a: pl.BlockSpec(a.shape, lambda i, pt, ix: (0, 0))
    hbm = pl.BlockSpec(memory_space=pl.ANY)
    return pl.pallas_call(
        functools.partial(_ssel_kernel, n_gather=n_gather, nb=nb, pg_off=layer * n_phys, wc_off=layer * nb),
        grid_spec=pltpu.PrefetchScalarGridSpec(
            num_scalar_prefetch=2,
            grid=(1,),
            in_specs=[hbm, hbm, full(q), full(new_rows), full(win_new), full(gates), full(o_c)],
            out_specs=pl.BlockSpec((nb, 256), lambda i, pt, ix: (0, 0)),
            scratch_shapes=[pltpu.VMEM((2, n_gather, 128, PAGE_SIZE), F32), pltpu.VMEM((2, 128, wb), F32),
                            pltpu.SemaphoreType.DMA((2,)), pltpu.SemaphoreType.DMA((2,))],
        ),
        out_shape=jax.ShapeDtypeStruct((nb, 256), F32),
        compiler_params=_cparams(("arbitrary",)),
    )(page_table, idx, cache_pairs, win_cache, q, new_rows, win_new, gates, o_c)


def _to_col(row, eye):
    return jnp.sum(jnp.where(eye, jnp.broadcast_to(row, eye.shape), 0.0), axis=1, keepdims=True)


def _srest_kernel(hq_ref, hf_ref, hi_ref, qm_ref, st_ref, cm_ref, la_ref, lc_ref, oml_ref, on_ref,
                  ohg_ref, omem_ref, stn_ref):
    q = hq_ref[0]
    fr = hf_ref[0]
    v = hi_ref[0]
    logf = _log_decay(fr, la_ref[...], lc_ref[...])
    f = jnp.exp(logf)
    k = oml_ref[...] * _sigmoid(-fr)
    eye = (lax.broadcasted_iota(jnp.int32, (GROUP_W, GROUP_W), 0)
           == lax.broadcasted_iota(jnp.int32, (GROUP_W, GROUP_W), 1))
    qc, fc, kc = _to_col(q, eye), _to_col(f, eye), _to_col(k, eye)
    vt = jnp.concatenate([jnp.broadcast_to(v[:, h * HEAD_DIM:(h + 1) * HEAD_DIM], (HEAD_DIM, HEAD_DIM))
                          for h in range(N_HEADS)], axis=0)
    s_new = fc * st_ref[0] + kc * vt
    stn_ref[0] = s_new
    x = qc * s_new
    on = on_ref[...]
    outs = []
    for h in range(N_HEADS):
        oh = jnp.sum(x[h * HEAD_DIM:(h + 1) * HEAD_DIM], axis=0, keepdims=True)
        ms = jnp.mean(oh * oh, axis=1, keepdims=True)
        outs.append(oh * lax.rsqrt(ms + EPS) * on[:, h * HEAD_DIM:(h + 1) * HEAD_DIM])
    ohg_ref[0] = jnp.concatenate(outs, axis=1)

    cm = cm_ref[0]
    qrow = qm_ref[0] * QK_SCALE
    r8 = lax.broadcasted_iota(jnp.int32, (8, GROUP_W), 0)
    lh8 = _lane_head((8, GROUP_W))
    qbd = jnp.where(r8 == lh8, jnp.broadcast_to(qrow, (8, GROUP_W)), 0.0)
    s = _hdot(qbd, cm[0:GROUP_W])
    s = s - s.max(axis=1, keepdims=True)
    p = jnp.exp(s)
    p = p / p.sum(axis=1, keepdims=True)
    o8 = _hdot_nt(p, cm[GROUP_W:2 * GROUP_W])
    omem_ref[0] = jnp.sum(jnp.where(r8 == lh8, o8, 0.0), axis=0, keepdims=True)


def _srest_call(hq, hf, hi, qm, state, cmem, la, lc, oml, on, *, layer):
    nb = hq.shape[0]
    per_b = lambda rows, wd: pl.BlockSpec((1, rows, wd), lambda b: (b, 0, 0))
    per_lb = lambda rows, wd: pl.BlockSpec((1, rows, wd), lambda b: (layer * nb + b, 0, 0))
    full = lambda a: pl.BlockSpec(a.shape, lambda b: (0, 0))
    return pl.pallas_call(
        _srest_kernel,
        grid=(nb,),
        in_specs=[per_b(1, 256)] * 4 + [per_lb(GROUP_W, HEAD_DIM), per_lb(2 * GROUP_W, N_MEM),
                                        full(la), full(lc), full(oml), full(on)],
        out_specs=[per_b(1, 256), per_b(1, 256), per_b(GROUP_W, HEAD_DIM)],
        out_shape=[jax.ShapeDtypeStruct((nb, 1, 256), F32)] * 2 + [jax.ShapeDtypeStruct((nb, GROUP_W, HEAD_DIM), F32)],
        compiler_params=_cparams(("parallel",)),
    )(hq, hf, hi, qm, state, cmem, la, lc, oml, on)


def _sfinal_kernel(x_ref, onsa_ref, ohg_ref, u_ref, buf_ref, omem_ref, z_ref, pw_ref, ps_ref, wo_ref, y_ref):
    u = u_ref[...]
    lane_g = _lane_head(u.shape)
    run = u
    pooled = jnp.zeros(u.shape, F32)
    nxt = 1
    for gi, wsz in enumerate(POOL_SIZES):
        for sh in range(nxt, wsz):
            run = run + buf_ref[0, POOL_BUF - sh]
        nxt = wsz
        pooled = jnp.where(lane_g == gi, run / float(wsz), pooled)
    o_pool = _hdot(pooled - u, pw_ref[...]) * ps_ref[...]
    z = z_ref[...]
    y = jnp.concatenate([onsa_ref[...], ohg_ref[...], o_pool, omem_ref[...]], axis=1) * (z * _sigmoid(z))
    y_ref[...] = x_ref[...] + _hdot(y, wo_ref[...])


def _sfinal_call(x, onsa, ohg, u, buf, omem, z, pw, ps, wo, *, layer):
    full = lambda a: pl.BlockSpec(a.shape, lambda i: (0, 0))
    buf_spec = pl.BlockSpec((1,) + buf.shape[1:], lambda i: (layer, 0, 0, 0))
    return pl.pallas_call(
        _sfinal_kernel,
        grid=(1,),
        in_specs=[full(x), full(onsa), full(ohg), full(u), buf_spec, full(omem), full(z), full(pw), full(ps), full(wo)],
        out_specs=full(x),
        out_shape=jax.ShapeDtypeStruct(x.shape, F32),
        compiler_params=_cparams(("arbitrary",)),
    )(x, onsa, ohg, u, buf, omem, z, pw, ps, wo)


def _block_diag_ones(w):
    i = np.arange(w)
    return jnp.asarray((i[:, None] // HEAD_DIM) == (i[None, :] // HEAD_DIM), BF16)


def _rope_tables(pos):
    half = HEAD_DIM // 2
    inv = ROPE_THETA ** (-jnp.arange(half, dtype=jnp.float32) / half)
    ang = pos.astype(jnp.float32)[:, None] * inv[None, :]
    cos = jnp.cos(ang)
    sin = jnp.sin(ang)
    cos64 = jnp.concatenate([cos, cos], axis=1)
    sin64 = jnp.concatenate([-sin, sin], axis=1)
    one = jnp.ones_like(cos64)
    zero = jnp.zeros_like(sin64)
    cq = jnp.tile(cos64, (1, N_HEADS))
    sq = jnp.tile(sin64, (1, N_HEADS))
    ckv = jnp.concatenate([cos64, one] * 3, axis=1)
    skv = jnp.concatenate([sin64, zero] * 3, axis=1)
    return cq, sq, ckv, skv


def _layer_params(l, norm_g, w_in, w_out, nsa_qn, nsa_kn, cmp_pe, cmp_w1, cmp_w2, lbs, hg_on, pool_w, pool_scale,
                  mem_norm, w_mem_kv, mem_qn, mem_kn):
    w = w_in[l]
    w_pad = jnp.concatenate([w[:, :652], jnp.zeros((D_MODEL, 116), F32), w[:, 652:]], axis=1)
    ones64 = jnp.ones((HEAD_DIM,), F32)
    kn = jnp.concatenate([nsa_kn[l, 0], ones64, nsa_kn[l, 1], ones64, nsa_kn[l, 2], ones64])[None, :]
    w1 = cmp_w1[l].reshape(2, L_CMP, HEAD_DIM, CMP_HID)
    w1e = jnp.zeros((L_CMP, 4, HEAD_DIM, 2 * CMP_HID), F32)
    w1e = w1e.at[:, 0, :, :CMP_HID].set(w1[0]).at[:, 1, :, CMP_HID:].set(w1[1])
    w2e = jnp.zeros((2 * CMP_HID, 128), F32)
    w2e = w2e.at[:CMP_HID, :HEAD_DIM].set(cmp_w2[l, 0]).at[CMP_HID:, HEAD_DIM:].set(cmp_w2[l, 1])
    pe = jnp.zeros((L_CMP, 4, HEAD_DIM), F32).at[:, 0].set(cmp_pe[l, 0]).at[:, 1].set(cmp_pe[l, 1])
    wr = jnp.zeros((L_CMP, 2, HEAD_DIM, 2 * CMP_HID), F32)
    wr = wr.at[:, 0, :, :CMP_HID].set(w1[0]).at[:, 1, :, CMP_HID:].set(w1[1])
    pw = jnp.zeros((GROUP_W, GROUP_W), F32)
    for g in range(len(POOL_SIZES)):
        pw = pw.at[g * 64:(g + 1) * 64, g * 64:(g + 1) * 64].set(pool_w[l, g])
    lb = lbs[l][None, :]
    return dict(
        g=norm_g[l][None, :], w32=w_pad, w16=w_pad.astype(BF16), wo32=w_out[l], wo16=w_out[l].astype(BF16),
        qn=jnp.tile(nsa_qn[l], N_HEADS)[None, :], kn=kn, mqn=jnp.tile(mem_qn[l], N_HEADS)[None, :],
        pe=pe.reshape(1, L_CMP * 256), w1e=w1e.reshape(L_CMP * 256, 2 * CMP_HID).astype(BF16), w2e=w2e.astype(BF16),
        pe2=jnp.tile(jnp.concatenate([cmp_pe[l, 0].T, cmp_pe[l, 1].T], axis=0), (1, 2 * PAGE_SIZE // L_CMP)),
        wr2=wr.reshape(L_CMP // 2, 256, 2 * CMP_HID).astype(BF16),
        la=jnp.log(lb), lc=jnp.log1p(-lb), oml=1.0 - lb, on=hg_on[l][None, :],
        pw32=pw, pw16=pw.astype(BF16), ps=pool_scale[l][None, :],
        mg=mem_norm[l][None, :], wm16=w_mem_kv[l].astype(BF16), mkn=jnp.tile(mem_kn[l], N_HEADS)[None, :],
    )


def _prompt_layer(x, mem, p, tabs, consts):
    b, t, _ = x.shape
    bd256, bd384, mk, _ = consts
    outs = _proj_call(x.reshape(b * t, D_MODEL), p['g'], p['w16'], *tabs, p['qn'], p['kn'], p['mqn'], bd256, bd384,
                      tm=512, t_tiles=t // 512, precise=False)
    q, nsa, win, gates, hq, hf, hi, u, qm, z = [o.reshape(b, t, -1) for o in outs]
    n_blk = t // L_CMP
    cc = _cmp_call(nsa.reshape(b * n_blk, L_CMP * 256), p['pe'], p['w1e'], p['w2e'],
                   tr=min(256, b * n_blk), n_rows=b * n_blk, row_off=0)
    cc = jnp.pad(cc.reshape(b, n_blk, 128), ((0, 0), (0, 128 - n_blk), (0, 0)))
    o_nsa = _nsa_call(q, nsa, win, gates, cc, tq=NSA_TQ)
    o_hg, st = _hgrn_call(hq, hf, hi, p['la'], p['lc'], p['oml'], p['on'], mk, bd256)
    mkv = _memkv_call(mem, p['mg'], p['wm16'], p['mkn'], bd256)
    y = _final_call(x, o_nsa, o_hg, u, qm, z, mkv, p['pw16'], p['ps'], p['wo16'], tm=min(512, t))
    st5 = st.reshape(b, N_HEADS, HEAD_DIM, N_HEADS, HEAD_DIM)
    s_new = jnp.stack([st5[:, h, :, h, :] for h in range(N_HEADS)], axis=1)
    s_new = jnp.swapaxes(s_new, 2, 3)
    wb = min(WINDOW, t)
    return (y, nsa.reshape(b, t, 4, 1, HEAD_DIM), win[:, t - wb:].reshape(b, wb, 2, 1, HEAD_DIM), s_new,
            u[:, t - POOL_BUF:], mkv.reshape(b, N_MEM, 2, N_HEADS, HEAD_DIM))


def _sample_layer(l, x, p, tabs, consts, cache_pages, cache_pairs, win_cache, state, pool_buf, cmem, page_table,
                  n_phys):
    nb = x.shape[0]
    bd256, bd384, _, perm = consts
    outs = _proj_call(x, p['g'], p['w32'], *tabs, p['qn'], p['kn'], p['mqn'], bd256, bd384,
                      tm=nb, t_tiles=1, precise=True)
    q, nsa, win, gates, hq, hf, hi, u, qm, z = outs
    npg = page_table.shape[1]
    cc = _cmp_pages_call(page_table, cache_pages, p['pe2'], perm, p['wr2'], p['w2e'], pp=min(64, npg),
                         page_off=l * n_phys)
    idx, o_c = _scmp_call(q, cc.reshape(nb * npg, 512), nb=nb, npg=npg)
    r3 = lambda a: a.reshape(nb, 1, -1)
    o_nsa = _ssel_call(page_table, idx, cache_pairs, q, nsa, win_cache, win, gates, o_c, layer=l, n_phys=n_phys)
    o_hg, o_mem, s_new = _srest_call(r3(hq), r3(hf), r3(hi), r3(qm), state, cmem,
                                     p['la'], p['lc'], p['oml'], p['on'], layer=l)
    y = _sfinal_call(x, o_nsa.reshape(nb, 256), o_hg.reshape(nb, 256), u, pool_buf, o_mem.reshape(nb, 256), z,
                     p['pw32'], p['ps'], p['wo32'], layer=l)
    return y, nsa, win, s_new.reshape(nb, N_HEADS, HEAD_DIM, HEAD_DIM), u


def kernel(x_prompt, x_sample, mem_prompt, cache_nsa, cache_nsa_win, state_hgrn, state_pool, cache_mem,
           page_table, norm_g, w_in, w_out, nsa_qn, nsa_kn, cmp_pe, cmp_w1, cmp_w2, hg_lb, hg_on,
           pool_w, pool_scale, mem_norm, w_mem_kv, mem_qn, mem_kn):
    depth = w_in.shape[0]
    b, t, _ = x_prompt.shape
    nb = x_sample.shape[0]
    past_len = page_table.shape[1] * PAGE_SIZE
    n_phys = cache_nsa.shape[1]
    wb = cache_nsa_win.shape[2]

    lbs = jnp.cumsum(jax.nn.softmax(hg_lb.astype(jnp.float32), axis=0), axis=0)
    lbs = lbs - lbs[0:1]
    consts = (_block_diag_ones(256), _block_diag_ones(384), jnp.asarray(_hgrn_consts(), F32), _pair_perm())
    tabs_p = _rope_tables(jnp.arange(t))
    tabs_s = tuple(jnp.broadcast_to(a, (nb, a.shape[1])) for a in _rope_tables(past_len + jnp.arange(1)))

    cache_t = jnp.transpose(cache_nsa, (0, 1, 3, 4, 5, 2))
    cache_pages = cache_t.reshape(depth * n_phys, 4 * HEAD_DIM, PAGE_SIZE)
    cache_pairs = cache_t.reshape(depth * n_phys * 2, 2 * HEAD_DIM, PAGE_SIZE)
    win_cache = jnp.transpose(cache_nsa_win, (0, 1, 3, 4, 5, 2)).reshape(depth * nb, 2 * HEAD_DIM, wb)
    state = state_hgrn.reshape(depth * nb, GROUP_W, HEAD_DIM)
    pool_buf = jnp.transpose(state_pool, (0, 2, 1, 3))
    cmem = jnp.transpose(cache_mem, (0, 1, 3, 4, 5, 2)).reshape(depth * nb, 2 * GROUP_W, N_MEM)

    xp, xs = x_prompt, x_sample.reshape(nb, D_MODEL)
    acc = [[] for _ in range(9)]
    for l in range(depth):
        p = _layer_params(l, norm_g, w_in, w_out, nsa_qn, nsa_kn, cmp_pe, cmp_w1, cmp_w2, lbs, hg_on, pool_w,
                          pool_scale, mem_norm, w_mem_kv, mem_qn, mem_kn)
        xp, a, bw, c, d, e = _prompt_layer(xp, mem_prompt, p, tabs_p, consts)
        xs, sa, sw, sc, sd = _sample_layer(l, xs, p, tabs_s, consts, cache_pages, cache_pairs, win_cache, state,
                                           pool_buf, cmem, page_table, n_phys)
        new_win_s = jnp.concatenate([cache_nsa_win[l][:, 1:], sw.reshape(nb, 1, 2, 1, HEAD_DIM)], axis=1)[:, -wb:]
        new_pool_s = jnp.concatenate([state_pool[l][:, 1:], sd.reshape(nb, 1, GROUP_W)], axis=1)
        for lst, val in zip(acc, (a, bw, c, d, e, sa.reshape(nb, 1, 4, 1, HEAD_DIM), new_win_s, sc, new_pool_s)):
            lst.append(val)
    return (xp, xs.reshape(nb, 1, D_MODEL)) + tuple(jnp.stack(v) for v in acc)
```

```python
import functools

import numpy as np
import jax
import jax.numpy as jnp
from jax import lax
from jax.experimental import pallas as pl
from jax.experimental.pallas import tpu as pltpu

D_MODEL = 1024
HEAD_DIM = 64
GROUP_W = 256
N_HEADS = 4
L_CMP = 32
L_SLC = 64
N_SEL = 16
WINDOW = 512
CMP_HID = 256
POOL_SIZES = (2, 4, 8, 16)
POOL_BUF = 15
N_MEM = 256
PAGE_SIZE = 128
ROPE_THETA = 10000.0
EPS = 1e-6
QK_SCALE = HEAD_DIM ** -0.5

PROJ_W = 3072
HG_CHUNK = 256
NSA_TQ = 256
NEG = -1e30
VMEM_LIMIT = 56 * 1024 * 1024

F32 = jnp.float32
BF16 = jnp.bfloat16
HI = lax.Precision.HIGHEST


def _cparams(sem):
    return pltpu.CompilerParams(dimension_semantics=sem, vmem_limit_bytes=VMEM_LIMIT)


def _bdot(a, b):
    return jnp.dot(a.astype(BF16), b.astype(BF16), preferred_element_type=F32)


def _bdot_nt(a, b):
    return lax.dot_general(a.astype(BF16), b.astype(BF16), (((1,), (1,)), ((), ())), preferred_element_type=F32)


def _hdot(a, b):
    return jnp.dot(a, b, precision=HI, preferred_element_type=F32)


def _hdot_nt(a, b):
    return lax.dot_general(a, b, (((1,), (1,)), ((), ())), precision=HI, preferred_element_type=F32)


def _dot2(a, b16):
    hi = a.astype(BF16)
    lo = (a - hi.astype(F32)).astype(BF16)
    return jnp.dot(hi, b16, preferred_element_type=F32) + jnp.dot(lo, b16, preferred_element_type=F32)


def _dot2_left(a16, b):
    hi = b.astype(BF16)
    lo = (b - hi.astype(F32)).astype(BF16)
    return jnp.dot(a16, hi, preferred_element_type=F32) + jnp.dot(a16, lo, preferred_element_type=F32)


def _seg_rms(x, bd16, gain):
    ms = _dot2(x * x, bd16) * (1.0 / HEAD_DIM)
    return x * lax.rsqrt(ms + EPS) * gain


def _rope(x, cos, sin_signed):
    w = x.shape[1]
    lane = lax.broadcasted_iota(jnp.int32, x.shape, 1)
    first = (lane & 32) == 0
    rot = jnp.where(first, pltpu.roll(x, w - 32, 1), pltpu.roll(x, 32, 1))
    return x * cos + rot * sin_signed


def _sigmoid(x):
    return jax.nn.sigmoid(x)


def _lane_head(shape):
    return lax.broadcasted_iota(jnp.int32, shape, 1) // HEAD_DIM


def _proj_kernel(x_ref, g_ref, w_ref, cq_ref, sq_ref, ckv_ref, skv_ref, qn_ref, kn_ref, mqn_ref, bd256_ref, bd384_ref,
                 q_out, nsa_out, win_out, gate_out, hq_out, hf_out, hi_out, u_out, qm_out, z_out, *, precise):
    x = x_ref[...]
    ms = jnp.mean(x * x, axis=-1, keepdims=True)
    h = x * lax.rsqrt(ms + EPS) * g_ref[...]
    proj = _hdot(h, w_ref[...]) if precise else _bdot(h, w_ref[...])
    bd256 = bd256_ref[...]
    q = _rope(_seg_rms(proj[:, 0:256], bd256, qn_ref[...]), cq_ref[...], sq_ref[...])
    q_out[...] = q
    kv = proj[:, 256:640]
    kvn = _rope(_seg_rms(kv, bd384_ref[...], kn_ref[...]), ckv_ref[...], skv_ref[...])
    lane = lax.broadcasted_iota(jnp.int32, kv.shape, 1)
    is_key = ((lane // HEAD_DIM) & 1) == 0
    kv = jnp.where(is_key, kvn, kv)
    nsa_out[...] = kv[:, 0:256]
    win_out[...] = kv[:, 256:384]
    gate_out[...] = _sigmoid(proj[:, 640:768])
    hq_out[...] = proj[:, 768:1024]
    hf_out[...] = proj[:, 1024:1280]
    hi_out[...] = proj[:, 1280:1536]
    u_out[...] = proj[:, 1536:1792]
    qm_out[...] = _seg_rms(proj[:, 1792:2048], bd256, mqn_ref[...])
    z_out[...] = proj[:, 2048:3072]


def _proj_call(x, g, w, cq, sq, ckv, skv, qn, kn, mqn, bd256, bd384, *, tm, t_tiles, precise):
    n = x.shape[0]
    row = lambda wd: pl.BlockSpec((tm, wd), lambda i: (i, 0))
    tab = lambda wd: pl.BlockSpec((tm, wd), lambda i: (i % t_tiles, 0))
    full = lambda a: pl.BlockSpec(a.shape, lambda i: (0, 0))
    widths = (256, 256, 128, 128, 256, 256, 256, 256, 256, 1024)
    return pl.pallas_call(
        functools.partial(_proj_kernel, precise=precise),
        grid=(n // tm,),
        in_specs=[row(D_MODEL), full(g), full(w), tab(256), tab(256), tab(384), tab(384),
                  full(qn), full(kn), full(mqn), full(bd256), full(bd384)],
        out_specs=[row(wd) for wd in widths],
        out_shape=[jax.ShapeDtypeStruct((n, wd), F32) for wd in widths],
        compiler_params=_cparams(("parallel",)),
    )(x, g, w, cq, sq, ckv, skv, qn, kn, mqn, bd256, bd384)


def _cmp_kernel(x_ref, pe_ref, w1_ref, w2_ref, o_ref):
    x = x_ref[...] + pe_ref[...]
    h = _bdot(x, w1_ref[...])
    h = h * _sigmoid(h)
    o_ref[...] = _bdot(h, w2_ref[...])


def _cmp_call(x, pe, w1, w2, *, tr, n_rows, row_off):
    off = row_off // tr
    full = lambda a: pl.BlockSpec(a.shape, lambda i: (0, 0))
    return pl.pallas_call(
        _cmp_kernel,
        grid=(n_rows // tr,),
        in_specs=[pl.BlockSpec((tr, x.shape[1]), lambda i: (i + off, 0)), full(pe), full(w1), full(w2)],
        out_specs=pl.BlockSpec((tr, 128), lambda i: (i, 0)),
        out_shape=jax.ShapeDtypeStruct((n_rows, 128), F32),
        compiler_params=_cparams(("parallel",)),
    )(x, pe, w1, w2)


def _cmp_pages_kernel(pt_ref, pages_hbm, pe_ref, perm_ref, wr_ref, w2_ref, o_ref, pg_buf, xs_scr, sem,
                      *, pp, nb, npg, pg_off):
    per_seq = npg // pp
    n_out = pp * (PAGE_SIZE // L_CMP)
    pe2 = pe_ref[...]
    perm = perm_ref[...]

    def copies(step, slot):
        b = step // per_seq
        i0 = (step - b * per_seq) * pp
        return [pltpu.make_async_copy(pages_hbm.at[pg_off + pt_ref[b, i0 + j], pl.ds(0, 2 * HEAD_DIM), :],
                                      pg_buf.at[slot, j], sem.at[slot]) for j in range(pp)]

    for cp in copies(0, 0):
        cp.start()

    def trip(step, carry):
        slot = step % 2

        @pl.when(step + 1 < nb * per_seq)
        def _():
            for cp in copies(step + 1, 1 - slot):
                cp.start()

        for cp in copies(step, slot):
            cp.wait()
        for i in range(pp // 2):
            m2 = jnp.concatenate([pg_buf[slot, 2 * i], pg_buf[slot, 2 * i + 1]], axis=1) + pe2
            xp = lax.dot_general(perm, m2.astype(BF16), (((1,), (1,)), ((), ())), preferred_element_type=F32)
            xs_scr[:, 8 * i:8 * i + 8, :] = xp.reshape(L_CMP, 8, 128)
        acc = jnp.zeros((n_out, 2 * CMP_HID), F32)
        for r2 in range(L_CMP // 2):
            slab = jnp.concatenate([xs_scr[2 * r2], xs_scr[2 * r2 + 1]], axis=1)
            acc = acc + _bdot(slab, wr_ref[r2])
        h = acc * _sigmoid(acc)
        o_ref[pl.ds(pl.multiple_of(step * n_out, n_out), n_out), :] = _bdot(h, w2_ref[...])
        return carry

    lax.fori_loop(0, nb * per_seq, trip, 0)


def _cmp_pages_call(page_table, pages, pe2, perm, wr2, w2, *, pp, page_off):
    nb, npg = page_table.shape
    n_out = pp * (PAGE_SIZE // L_CMP)
    full = lambda a: pl.BlockSpec(a.shape, lambda i, pt: (0,) * a.ndim)
    n_rows = nb * npg * (PAGE_SIZE // L_CMP)
    return pl.pallas_call(
        functools.partial(_cmp_pages_kernel, pp=pp, nb=nb, npg=npg, pg_off=page_off),
        grid_spec=pltpu.PrefetchScalarGridSpec(
            num_scalar_prefetch=1,
            grid=(1,),
            in_specs=[pl.BlockSpec(memory_space=pl.ANY), full(pe2), full(perm), full(wr2), full(w2)],
            out_specs=pl.BlockSpec((n_rows, 128), lambda i, pt: (0, 0)),
            scratch_shapes=[pltpu.VMEM((2, pp, 2 * HEAD_DIM, PAGE_SIZE), F32), pltpu.VMEM((L_CMP, n_out, 128), F32),
                            pltpu.SemaphoreType.DMA((2,))],
        ),
        out_shape=jax.ShapeDtypeStruct((n_rows, 128), F32),
        compiler_params=_cparams(("arbitrary",)),
    )(page_table, pages, pe2, perm, wr2, w2)


def _pair_perm():
    a = np.zeros((L_CMP, 8, 2, PAGE_SIZE), np.float32)
    for r in range(L_CMP):
        for pq in range(2):
            for nb in range(PAGE_SIZE // L_CMP):
                a[r, 4 * pq + nb, pq, L_CMP * nb + r] = 1.0
    return jnp.asarray(a.reshape(L_CMP * 8, 2 * PAGE_SIZE), BF16)


def _topk_mask(score, k, idx_f, axis=1):
    sel = jnp.zeros(score.shape, jnp.bool_)
    s = score
    big = jnp.float32(1e9)
    picks = []
    for _ in range(k):
        m = jnp.max(s, axis=axis, keepdims=True)
        cand = jnp.where(s == m, idx_f, big)
        pick = jnp.min(cand, axis=axis, keepdims=True)
        hit = idx_f == pick
        sel = sel | hit
        s = jnp.where(hit, -jnp.inf, s)
        picks.append(pick)
    return sel, picks


def _flash_step(carry, k16, vt16, bias, qs_scr):
    return _flash_update(carry, _scores(k16, qs_scr), vt16, bias)


def _scores(k16, qs_scr):
    return lax.dot_general(k16, qs_scr[...], (((1,), (1,)), ((), ())), preferred_element_type=F32)


def _flash_update(carry, s, vt16, bias):
    m, l, acc = carry
    if bias is not None:
        s = s + (jnp.concatenate([bias] * N_HEADS, axis=1) if bias.ndim == 2 else bias)
    m_new = jnp.maximum(m, jnp.max(s, axis=0, keepdims=True))
    alpha = jnp.exp(m - m_new)
    p = jnp.exp(s - m_new)
    l = alpha * l + jnp.sum(p, axis=0, keepdims=True)
    acc = alpha * acc + jnp.dot(vt16, p.astype(BF16), preferred_element_type=F32)
    return m_new, l, acc


def _nsa_kernel(q_ref, nsa_ref, win_ref, gate_ref, cc_ref, o_ref, qs_scr, vt_scr, wt_scr, *, tq):
    i = pl.program_id(1)
    t_len = nsa_ref.shape[1]
    assert WINDOW == 2 * tq and t_len % tq == 0
    n_cmp = HEAD_DIM

    @pl.when(i == 0)
    def _():
        for c in range(t_len // tq):
            sl = slice(c * tq, (c + 1) * tq)
            vt_scr[c] = nsa_ref[0, sl, 128:256].T.astype(BF16)
            wt_scr[c] = win_ref[0, sl, :].T.astype(BF16)

    t0 = i * tq
    r4 = N_HEADS * tq
    q = q_ref[0] * QK_SCALE
    for h in range(N_HEADS):
        qs_scr[h * tq:(h + 1) * tq, :] = q[:, h * HEAD_DIM:(h + 1) * HEAD_DIM].astype(BF16)
    t_pos = t0 + lax.broadcasted_iota(jnp.int32, (1, tq), 1)
    t_pos4 = t0 + lax.broadcasted_iota(jnp.int32, (1, r4), 1) % tq

    cc = cc_ref[0]
    kcc16 = cc[0:n_cmp, 0:HEAD_DIM].astype(BF16)
    vcct16 = cc.T[HEAD_DIM:2 * HEAD_DIM, 0:n_cmp].astype(BF16)
    done = (lax.broadcasted_iota(jnp.int32, (n_cmp, r4), 0) + 1) * L_CMP - 1 <= t_pos4
    s = lax.dot_general(kcc16, qs_scr[...], (((1,), (1,)), ((), ())), preferred_element_type=F32)
    s = jnp.where(done, s, NEG)
    m = jnp.max(s, axis=0, keepdims=True)
    m = jnp.where(m < 0.5 * NEG, 0.0, m)
    p = jnp.where(done, jnp.exp(s - m), 0.0)
    p = p / jnp.maximum(jnp.sum(p, axis=0, keepdims=True), 1e-30)
    o_c = jnp.dot(vcct16, p.astype(BF16), preferred_element_type=F32)
    imp = p[:, 0:tq] + p[:, tq:2 * tq] + p[:, 2 * tq:3 * tq] + p[:, 3 * tq:4 * tq]

    n_idx = lax.broadcasted_iota(jnp.int32, (n_cmp, tq), 0)
    imp2 = imp + pltpu.roll(imp, n_cmp - 1, 0)
    blk_t = t_pos // L_SLC
    j = n_idx // 2
    valid = ((n_idx & 1) == 0) & (j < t_len // L_SLC)
    avail = valid & (j <= blk_t)
    forced = (j == 0) | (j == blk_t) | (j == blk_t - 1)
    score = jnp.where(avail, jnp.where(forced, jnp.inf, imp2), -jnp.inf)
    sel, _ = _topk_mask(score, N_SEL, n_idx.astype(F32), axis=0)
    sel16 = jnp.where(sel & avail, 1.0, 0.0).astype(BF16)

    key_l = lax.broadcasted_iota(jnp.int32, (tq, tq), 0)
    qry_l = lax.broadcasted_iota(jnp.int32, (tq, tq), 1)
    causal = jnp.where(key_l <= qry_l, 0.0, NEG)
    tail = jnp.where(key_l > qry_l, 0.0, NEG)

    def sel_bias(c):
        ek = c * tq + lax.broadcasted_iota(jnp.int32, (tq, n_cmp), 0)
        en = lax.broadcasted_iota(jnp.int32, (tq, n_cmp), 1)
        expand = jnp.where(en == 2 * (ek // L_SLC), 1.0, 0.0).astype(BF16)
        msel = jnp.dot(expand, sel16, preferred_element_type=F32)
        return (msel - 1.0) * (-NEG)

    def chunk(ref, vt, c, lo):
        start = pl.multiple_of(c * tq, tq)
        return ref[0, pl.ds(start, tq), lo:lo + HEAD_DIM].astype(BF16), vt[c, HEAD_DIM:2 * HEAD_DIM, :]

    init = (jnp.full((1, r4), NEG, F32), jnp.zeros((1, r4), F32), jnp.zeros((HEAD_DIM, r4), F32))

    def sel_body(c, carry):
        k16, vt16 = chunk(nsa_ref, vt_scr, c, 128)
        return _flash_step(carry, k16, vt16, sel_bias(c), qs_scr)

    st_s = lax.fori_loop(0, i, sel_body, init)
    k16, vt16 = chunk(nsa_ref, vt_scr, i, 128)
    st_s = _flash_step(st_s, k16, vt16, sel_bias(i) + causal, qs_scr)

    k16, vt16 = chunk(win_ref, wt_scr, i, 0)
    st_w = _flash_step(init, k16, vt16, causal, qs_scr)
    k16, vt16 = chunk(win_ref, wt_scr, jnp.maximum(i - 1, 0), 0)
    st_w = _flash_step(st_w, k16, vt16, jnp.where(i >= 1, 0.0, NEG), qs_scr)
    k16, vt16 = chunk(win_ref, wt_scr, jnp.maximum(i - 2, 0), 0)
    st_w = _flash_step(st_w, k16, vt16, tail + jnp.where(i >= 2, 0.0, NEG), qs_scr)

    gt = gate_ref[0].T
    g_c, g_s, g_w = (jnp.concatenate([gt[3 * h + br:3 * h + br + 1] for h in range(N_HEADS)], axis=1)
                     for br in range(3))
    o_s = st_s[2] / jnp.maximum(st_s[1], 1e-30)
    o_w = st_w[2] / jnp.maximum(st_w[1], 1e-30)
    o = g_c * o_c + g_s * o_s + g_w * o_w
    o_ref[0] = jnp.concatenate([o[:, h * tq:(h + 1) * tq] for h in range(N_HEADS)], axis=0).T


def _nsa_call(q, nsa, win, gates, cc, *, tq):
    b, t, _ = q.shape
    n_ch = t // tq
    return pl.pallas_call(
        functools.partial(_nsa_kernel, tq=tq),
        grid=(b, n_ch),
        in_specs=[pl.BlockSpec((1, tq, 256), lambda bi, i: (bi, i, 0)),
                  pl.BlockSpec((1, t, 256), lambda bi, i: (bi, 0, 0)),
                  pl.BlockSpec((1, t, 128), lambda bi, i: (bi, 0, 0)),
                  pl.BlockSpec((1, tq, 128), lambda bi, i: (bi, i, 0)),
                  pl.BlockSpec((1, 128, 128), lambda bi, i: (bi, 0, 0))],
        out_specs=pl.BlockSpec((1, tq, 256), lambda bi, i: (bi, i, 0)),
        out_shape=jax.ShapeDtypeStruct((b, t, 256), F32),
        scratch_shapes=[pltpu.VMEM((N_HEADS * tq, HEAD_DIM), BF16), pltpu.VMEM((n_ch, 128, tq), BF16),
                        pltpu.VMEM((n_ch, 128, tq), BF16)],
        compiler_params=_cparams(("parallel", "arbitrary")),
    )(q, nsa, win, gates, cc)


def _hgrn_consts(c=HG_CHUNK):
    t = np.arange(c)[:, None]
    k = np.arange(c)[None, :]
    masks = []
    m = 1
    while m < c:
        upper = (t % (2 * m)) >= m
        masks.append(upper & ((k // (2 * m)) == t // (2 * m)) & ((k % (2 * m)) < m))
        m *= 2
    masks.append(t == k)
    return np.stack(masks).astype(np.float32)


def _block_row_bcast(p, m):
    c, w = p.shape
    if 2 * m >= 8:
        p3 = p.reshape(c // (2 * m), 2 * m, w)
        return jnp.broadcast_to(p3[:, m - 1:m, :], p3.shape).reshape(c, w)
    off = lax.broadcasted_iota(jnp.int32, p.shape, 0) % (2 * m)
    out = p
    for o in range(2 * m):
        if o != m - 1:
            out = jnp.where(off == o, pltpu.roll(p, (o - (m - 1)) % c, 0), out)
    return out


def _log_decay(fr, la, lc):
    ls = jnp.minimum(fr, 0.0) - jnp.log(1.0 + jnp.exp(-jnp.abs(fr)))
    b = lc + ls
    return jnp.maximum(la, b) + jnp.log(1.0 + jnp.exp(-jnp.abs(la - b)))


def _hgrn_kernel(hq_ref, hf_ref, hi_ref, la_ref, lc_ref, oml_ref, on_ref, mk_ref, bd_ref,
                 o_ref, st_ref, st_scr):
    c = HG_CHUNK
    jt = pl.program_id(1)

    @pl.when(jt == 0)
    def _():
        st_scr[...] = jnp.zeros_like(st_scr)

    q = hq_ref[0]
    fr = hf_ref[0]
    v = hi_ref[0]
    logf = _log_decay(fr, la_ref[...], lc_ref[...])
    k = oml_ref[...] * _sigmoid(-fr)
    row = lax.broadcasted_iota(jnp.int32, (c, GROUP_W), 0)
    pre = logf
    decays = []
    m = 1
    while m < c:
        mid = _block_row_bcast(pre, m)
        upper = (row % (2 * m)) >= m
        decays.append(jnp.exp(jnp.where(upper, pre, mid - pre)))
        pre = pre + jnp.where(upper, mid, 0.0)
        m *= 2
    gc = pre
    gl = gc[c - 1:c, :]
    lane_h = _lane_head((c, GROUP_W))
    n_lvl = mk_ref.shape[0]
    a_tot = [jnp.zeros((c, c), F32) for _ in range(N_HEADS)]
    for lv in range(n_lvl):
        qt = q * decays[lv] if lv < len(decays) else q
        kt = (k * decays[lv] if lv < len(decays) else k).astype(BF16)
        qs = jnp.concatenate([jnp.where(lane_h == h, qt, 0.0).astype(BF16) for h in range(N_HEADS)], axis=0)
        a_l = lax.dot_general(qs, kt, (((1,), (1,)), ((), ())), preferred_element_type=F32)
        mk = mk_ref[lv]
        for h in range(N_HEADS):
            a_tot[h] = a_tot[h] + mk * a_l[h * c:(h + 1) * c]
    a_all = jnp.concatenate(a_tot, axis=0)
    r = _bdot(a_all, v)
    o = jnp.zeros((c, GROUP_W), F32)
    for h in range(N_HEADS):
        o = o + jnp.where(lane_h == h, r[h * c:(h + 1) * c], 0.0)
    st = st_scr[...]
    o = o + _bdot_nt(q * jnp.exp(gc), st)
    khat = k * jnp.exp(gl - gc)
    upd = jnp.dot(v.T.astype(BF16), khat.astype(BF16), preferred_element_type=F32)
    row_h = lax.broadcasted_iota(jnp.int32, (GROUP_W, GROUP_W), 0) // HEAD_DIM
    col_h = _lane_head((GROUP_W, GROUP_W))
    st_new = jnp.exp(gl) * st + jnp.where(row_h == col_h, upd, 0.0)
    st_scr[...] = st_new
    st_ref[0] = st_new
    o_ref[0] = _seg_rms(o, bd_ref[...], on_ref[...])


def _hgrn_call(hq, hf, hi, la, lc, oml, on, mk, bd256):
    b, t, _ = hq.shape
    c = HG_CHUNK
    row = pl.BlockSpec((1, c, GROUP_W), lambda bi, j: (bi, j, 0))
    full2 = lambda a: pl.BlockSpec(a.shape, lambda bi, j: (0, 0))
    return pl.pallas_call(
        _hgrn_kernel,
        grid=(b, t // c),
        in_specs=[row, row, row, full2(la), full2(lc), full2(oml), full2(on),
                  pl.BlockSpec(mk.shape, lambda bi, j: (0, 0, 0)), full2(bd256)],
        out_specs=[row, pl.BlockSpec((1, GROUP_W, GROUP_W), lambda bi, j: (bi, 0, 0))],
        out_shape=[jax.ShapeDtypeStruct((b, t, GROUP_W), F32), jax.ShapeDtypeStruct((b, GROUP_W, GROUP_W), F32)],
        scratch_shapes=[pltpu.VMEM((GROUP_W, GROUP_W), F32)],
        compiler_params=_cparams(("parallel", "arbitrary")),
    )(hq, hf, hi, la, lc, oml, on, mk, bd256)


def _memkv_kernel(m_ref, g_ref, w_ref, kn_ref, bd_ref, o_ref):
    x = m_ref[0]
    ms = jnp.mean(x * x, axis=-1, keepdims=True)
    h = x * lax.rsqrt(ms + EPS) * g_ref[...]
    kv = _bdot(h, w_ref[...])
    o_ref[0, :, 0:GROUP_W] = _seg_rms(kv[:, 0:GROUP_W], bd_ref[...], kn_ref[...])
    o_ref[0, :, GROUP_W:2 * GROUP_W] = kv[:, GROUP_W:2 * GROUP_W]


def _memkv_call(mem, g, w, kn, bd256):
    b = mem.shape[0]
    full = lambda a: pl.BlockSpec(a.shape, lambda i: (0, 0))
    return pl.pallas_call(
        _memkv_kernel,
        grid=(b,),
        in_specs=[pl.BlockSpec((1, N_MEM, D_MODEL), lambda i: (i, 0, 0)), full(g), full(w), full(kn), full(bd256)],
        out_specs=pl.BlockSpec((1, N_MEM, 2 * GROUP_W), lambda i: (i, 0, 0)),
        out_shape=jax.ShapeDtypeStruct((b, N_MEM, 2 * GROUP_W), F32),
        compiler_params=_cparams(("parallel",)),
    )(mem, g, w, kn, bd256)


def _final_kernel(x_ref, onsa_ref, ohg_ref, u_ref, halo_ref, qm_ref, z_ref, mkv_ref, pw_ref, ps_ref, wo_ref,
                  y_ref, ext_scr, *, tm):
    jt = pl.program_id(1)
    u = u_ref[0]
    halo = halo_ref[0]
    ext_scr[0:16, :] = jnp.where(jt == 0, 0.0, halo)
    ext_scr[16:16 + tm, :] = u
    pos = jt * tm + lax.broadcasted_iota(jnp.int32, (tm, 1), 0)
    lane_g = _lane_head((tm, GROUP_W))
    pooled = jnp.zeros((tm, GROUP_W), F32)
    w = 1
    for gi, wsz in enumerate(POOL_SIZES):
        assert wsz == 2 * w
        ext_scr[w:16 + tm, :] = ext_scr[w:16 + tm, :] + ext_scr[0:16 + tm - w, :]
        w = wsz
        cnt = jnp.minimum(wsz, pos + 1).astype(F32)
        pooled = jnp.where(lane_g == gi, ext_scr[16:16 + tm, :] / cnt, pooled)
    o_pool = _bdot(pooled - u, pw_ref[...]) * ps_ref[...]

    qm = qm_ref[0] * QK_SCALE
    mkv = mkv_ref[0]
    heads = []
    for h in range(N_HEADS):
        sl = slice(h * HEAD_DIM, (h + 1) * HEAD_DIM)
        s = _bdot_nt(qm[:, sl], mkv[:, sl])
        s = s - jnp.max(s, axis=1, keepdims=True)
        p = jnp.exp(s)
        p = p / jnp.sum(p, axis=1, keepdims=True)
        heads.append(_bdot(p, mkv[:, GROUP_W + h * HEAD_DIM:GROUP_W + (h + 1) * HEAD_DIM]))
    o_mem = jnp.concatenate(heads, axis=1)

    z = z_ref[0]
    y = jnp.concatenate([onsa_ref[0], ohg_ref[0], o_pool, o_mem], axis=1) * (z * _sigmoid(z))
    y_ref[0] = x_ref[0] + _bdot(y, wo_ref[...])


def _final_call(x, onsa, ohg, u, qm, z, mkv, pw, ps, wo, *, tm):
    b, t, _ = x.shape
    row = lambda wd: pl.BlockSpec((1, tm, wd), lambda bi, j: (bi, j, 0))
    full2 = lambda a: pl.BlockSpec(a.shape, lambda bi, j: (0, 0))
    hb = tm // 16
    return pl.pallas_call(
        functools.partial(_final_kernel, tm=tm),
        grid=(b, t // tm),
        in_specs=[row(D_MODEL), row(GROUP_W), row(GROUP_W), row(GROUP_W),
                  pl.BlockSpec((1, 16, GROUP_W), lambda bi, j: (bi, jnp.maximum(j * hb - 1, 0), 0)),
                  row(GROUP_W), row(D_MODEL),
                  pl.BlockSpec((1, N_MEM, 2 * GROUP_W), lambda bi, j: (bi, 0, 0)),
                  full2(pw), full2(ps), full2(wo)],
        out_specs=row(D_MODEL),
        out_shape=jax.ShapeDtypeStruct((b, t, D_MODEL), F32),
        scratch_shapes=[pltpu.VMEM((tm + 16, GROUP_W), F32)],
        compiler_params=_cparams(("parallel", "arbitrary")),
    )(x, onsa, ohg, u, u, qm, z, mkv, pw, ps, wo)


def _head_rows(row):
    r8 = lax.broadcasted_iota(jnp.int32, (8, GROUP_W), 0)
    qm = jnp.where(r8 == _lane_head((8, GROUP_W)), jnp.broadcast_to(row, (8, GROUP_W)), 0.0)
    return qm[:, 0:64] + qm[:, 64:128] + qm[:, 128:192] + qm[:, 192:256]


def _scmp_kernel(q_ref, cc_ref, idx_ref, oc_ref, a_scr, b_scr, *, nb, npg):
    b = pl.program_id(0)
    gb = cc_ref[...]
    qh = _head_rows(q_ref[pl.ds(b, 1), :]) * QK_SCALE
    ss = [_hdot_nt(qh, gb[:, r * 128:r * 128 + HEAD_DIM]) for r in range(4)]
    m = ss[0].max(axis=1, keepdims=True)
    for r in range(1, 4):
        m = jnp.maximum(m, ss[r].max(axis=1, keepdims=True))
    ps = [jnp.exp(s - m) for s in ss]
    l = ps[0].sum(axis=1, keepdims=True)
    for r in range(1, 4):
        l = l + ps[r].sum(axis=1, keepdims=True)
    inv = 1.0 / jnp.maximum(l, 1e-30)
    ps = [p * inv for p in ps]
    oc = _hdot(ps[0], gb[:, HEAD_DIM:128])
    for r in range(1, 4):
        oc = oc + _hdot(ps[r], gb[:, r * 128 + HEAD_DIM:(r + 1) * 128])
    oc_ref[pl.ds(pl.multiple_of(b * 8, 8), 8), :] = oc
    hmask = lax.broadcasted_iota(jnp.int32, ss[0].shape, 0) < N_HEADS
    imps = [jnp.sum(jnp.where(hmask, p, 0.0), axis=0, keepdims=True) for p in ps]
    a_scr[pl.ds(b, 1), :] = imps[0] + imps[1]
    b_scr[pl.ds(b, 1), :] = imps[2] + imps[3]

    @pl.when(b == nb - 1)
    def _():
        score = jnp.concatenate([a_scr[...], b_scr[...]], axis=1)
        lane = lax.broadcasted_iota(jnp.int32, score.shape, 1)
        jidx = jnp.where(lane < npg, 2 * lane, 2 * (lane - npg) + 1)
        forced = (jidx == 0) | (jidx == 2 * npg - 1)
        score = jnp.where(forced, jnp.inf, score)
        _, picks = _topk_mask(score, N_SEL - 1, jidx.astype(F32))
        out_lane = lax.broadcasted_iota(jnp.int32, (nb, 128), 1)
        out = jnp.zeros((nb, 128), jnp.int32)
        for r, pk in enumerate(picks):
            out = jnp.where(out_lane == r, pk.astype(jnp.int32), out)
        idx_ref[...] = out


def _scmp_call(q, cc_rows, *, nb, npg):
    return pl.pallas_call(
        functools.partial(_scmp_kernel, nb=nb, npg=npg),
        grid=(nb,),
        in_specs=[pl.BlockSpec(q.shape, lambda b: (0, 0)), pl.BlockSpec((npg, 512), lambda b: (b, 0))],
        out_specs=[pl.BlockSpec((nb, 128), lambda b: (0, 0)), pl.BlockSpec((nb * 8, HEAD_DIM), lambda b: (0, 0))],
        out_shape=[jax.ShapeDtypeStruct((nb, 128), jnp.int32), jax.ShapeDtypeStruct((nb * 8, HEAD_DIM), F32)],
        scratch_shapes=[pltpu.VMEM((nb, 128), F32), pltpu.VMEM((nb, 128), F32)],
        compiler_params=_cparams(("arbitrary",)),
    )(q, cc_rows)


def _ssel_kernel(pt_ref, idx_ref, pairs_hbm, wc_hbm, q_ref, new_ref, wnew_ref, gate_ref, oc_ref, o_ref,
                 blk_buf, wc_buf, blk_sem, wc_sem, *, n_gather, nb, pg_off, wc_off):
    def copies(b, slot):
        out = []
        for r in range(n_gather):
            src = (pg_off + pt_ref[b, idx_ref[b, r] // 2]) * 2 + 1
            out.append(pltpu.make_async_copy(pairs_hbm.at[src], blk_buf.at[slot, r], blk_sem.at[slot]))
        out.append(pltpu.make_async_copy(wc_hbm.at[wc_off + b], wc_buf.at[slot], wc_sem.at[slot]))
        return out

    for cp in copies(0, 0):
        cp.start()
    lane_half = lax.broadcasted_iota(jnp.int32, (8, PAGE_SIZE), 1) // L_SLC

    def per_seq(b, carry):
        slot = b % 2

        @pl.when(b + 1 < nb)
        def _():
            for cp in copies(b + 1, 1 - slot):
                cp.start()

        for cp in copies(b, slot):
            cp.wait()
        qh = _head_rows(q_ref[pl.ds(b, 1), :]) * QK_SCALE
        new = new_ref[pl.ds(b, 1), :]
        s_n = jnp.sum(qh * new[:, 128:192], axis=1, keepdims=True)
        scores = []
        m = s_n
        for r in range(n_gather):
            half = idx_ref[b, r] % 2
            s = _hdot(qh, blk_buf[slot, r, 0:HEAD_DIM, :])
            s = jnp.where(lane_half == half, s, NEG)
            scores.append(s)
            m = jnp.maximum(m, s.max(axis=1, keepdims=True))
        p_n = jnp.exp(s_n - m)
        l = p_n
        acc = p_n * new[:, 192:256]
        for r in range(n_gather):
            p = jnp.exp(scores[r] - m)
            l = l + p.sum(axis=1, keepdims=True)
            acc = acc + _hdot_nt(p, blk_buf[slot, r, HEAD_DIM:2 * HEAD_DIM, :])
        o_s = acc / jnp.maximum(l, 1e-30)
        wnew = wnew_ref[pl.ds(b, 1), :]
        s_w = _hdot(qh, wc_buf[slot, 0:HEAD_DIM, :])
        keep = lax.broadcasted_iota(jnp.int32, s_w.shape, 1) >= s_w.shape[1] + 1 - WINDOW
        s_w = jnp.where(keep, s_w, NEG)
        s_wn = jnp.sum(qh * wnew[:, 0:HEAD_DIM], axis=1, keepdims=True)
        m_w = jnp.maximum(s_w.max(axis=1, keepdims=True), s_wn)
        p_w = jnp.where(keep, jnp.exp(s_w - m_w), 0.0)
        p_wn = jnp.exp(s_wn - m_w)
        l_w = p_w.sum(axis=1, keepdims=True) + p_wn
        o_w = (_hdot_nt(p_w, wc_buf[slot, HEAD_DIM:128, :]) + p_wn * wnew[:, HEAD_DIM:128]) / jnp.maximum(l_w, 1e-30)
        o_c = oc_ref[pl.ds(pl.multiple_of(b * 8, 8), 8), :]
        g = gate_ref[pl.ds(b, 1), :]
        outs = []
        for h in range(N_HEADS):
            outs.append(g[:, 3 * h:3 * h + 1] * o_c[h:h + 1] + g[:, 3 * h + 1:3 * h + 2] * o_s[h:h + 1]
                        + g[:, 3 * h + 2:3 * h + 3] * o_w[h:h + 1])
        o_ref[pl.ds(b, 1), :] = jnp.concatenate(outs, axis=1)
        return carry

    lax.fori_loop(0, nb, per_seq, 0)


def _ssel_call(page_table, idx, cache_pairs, q, new_rows, win_cache, win_new, gates, o_c, *, layer, n_phys):
    nb = q.shape[0]
    n_gather = N_SEL - 1
    wb = win_cache.shape[2]
    full = lambda a: pl.BlockSpec(a.shape, lambda i, pt, ix: (0, 0))
    hbm = pl.BlockSpec(memory_space=pl.ANY)
    return pl.pallas_call(
        functools.partial(_ssel_kernel, n_gather=n_gather, nb=nb, pg_off=layer * n_phys, wc_off=layer * nb),
        grid_spec=pltpu.PrefetchScalarGridSpec(
            num_scalar_prefetch=2,
            grid=(1,),
            in_specs=[hbm, hbm, full(q), full(new_rows), full(win_new), full(gates), full(o_c)],
            out_specs=pl.BlockSpec((nb, 256), lambda i, pt, ix: (0, 0)),
            scratch_shapes=[pltpu.VMEM((2, n_gather, 128, PAGE_SIZE), F32), pltpu.VMEM((2, 128, wb), F32),
                            pltpu.SemaphoreType.DMA((2,)), pltpu.SemaphoreType.DMA((2,))],
        ),
        out_shape=jax.ShapeDtypeStruct((nb, 256), F32),
        compiler_params=_cparams(("arbitrary",)),
    )(page_table, idx, cache_pairs, win_cache, q, new_rows, win_new, gates, o_c)


def _to_col(row, eye):
    return jnp.sum(jnp.where(eye, jnp.broadcast_to(row, eye.shape), 0.0), axis=1, keepdims=True)


def _srest_kernel(hq_ref, hf_ref, hi_ref, qm_ref, st_ref, cm_ref, la_ref, lc_ref, oml_ref, on_ref,
                  ohg_ref, omem_ref, stn_ref):
    q = hq_ref[0]
    fr = hf_ref[0]
    v = hi_ref[0]
    logf = _log_decay(fr, la_ref[...], lc_ref[...])
    f = jnp.exp(logf)
    k = oml_ref[...] * _sigmoid(-fr)
    eye = (lax.broadcasted_iota(jnp.int32, (GROUP_W, GROUP_W), 0)
           == lax.broadcasted_iota(jnp.int32, (GROUP_W, GROUP_W), 1))
    qc, fc, kc = _to_col(q, eye), _to_col(f, eye), _to_col(k, eye)
    vt = jnp.concatenate([jnp.broadcast_to(v[:, h * HEAD_DIM:(h + 1) * HEAD_DIM], (HEAD_DIM, HEAD_DIM))
                          for h in range(N_HEADS)], axis=0)
    s_new = fc * st_ref[0] + kc * vt
    stn_ref[0] = s_new
    x = qc * s_new
    on = on_ref[...]
    outs = []
    for h in range(N_HEADS):
        oh = jnp.sum(x[h * HEAD_DIM:(h + 1) * HEAD_DIM], axis=0, keepdims=True)
        ms = jnp.mean(oh * oh, axis=1, keepdims=True)
        outs.append(oh * lax.rsqrt(ms + EPS) * on[:, h * HEAD_DIM:(h + 1) * HEAD_DIM])
    ohg_ref[0] = jnp.concatenate(outs, axis=1)

    cm = cm_ref[0]
    qrow = qm_ref[0] * QK_SCALE
    r8 = lax.broadcasted_iota(jnp.int32, (8, GROUP_W), 0)
    lh8 = _lane_head((8, GROUP_W))
    qbd = jnp.where(r8 == lh8, jnp.broadcast_to(qrow, (8, GROUP_W)), 0.0)
    s = _hdot(qbd, cm[0:GROUP_W])
    s = s - s.max(axis=1, keepdims=True)
    p = jnp.exp(s)
    p = p / p.sum(axis=1, keepdims=True)
    o8 = _hdot_nt(p, cm[GROUP_W:2 * GROUP_W])
    omem_ref[0] = jnp.sum(jnp.where(r8 == lh8, o8, 0.0), axis=0, keepdims=True)


def _srest_call(hq, hf, hi, qm, state, cmem, la, lc, oml, on, *, layer):
    nb = hq.shape[0]
    per_b = lambda rows, wd: pl.BlockSpec((1, rows, wd), lambda b: (b, 0, 0))
    per_lb = lambda rows, wd: pl.BlockSpec((1, rows, wd), lambda b: (layer * nb + b, 0, 0))
    full = lambda a: pl.BlockSpec(a.shape, lambda b: (0, 0))
    return pl.pallas_call(
        _srest_kernel,
        grid=(nb,),
        in_specs=[per_b(1, 256)] * 4 + [per_lb(GROUP_W, HEAD_DIM), per_lb(2 * GROUP_W, N_MEM),
                                        full(la), full(lc), full(oml), full(on)],
        out_specs=[per_b(1, 256), per_b(1, 256), per_b(GROUP_W, HEAD_DIM)],
        out_shape=[jax.ShapeDtypeStruct((nb, 1, 256), F32)] * 2 + [jax.ShapeDtypeStruct((nb, GROUP_W, HEAD_DIM), F32)],
        compiler_params=_cparams(("parallel",)),
    )(hq, hf, hi, qm, state, cmem, la, lc, oml, on)


def _sfinal_kernel(x_ref, onsa_ref, ohg_ref, u_ref, buf_ref, omem_ref, z_ref, pw_ref, ps_ref, wo_ref, y_ref):
    u = u_ref[...]
    lane_g = _lane_head(u.shape)
    run = u
    pooled = jnp.zeros(u.shape, F32)
    nxt = 1
    for gi, wsz in enumerate(POOL_SIZES):
        for sh in range(nxt, wsz):
            run = run + buf_ref[0, POOL_BUF - sh]
        nxt = wsz
        pooled = jnp.where(lane_g == gi, run / float(wsz), pooled)
    o_pool = _hdot(pooled - u, pw_ref[...]) * ps_ref[...]
    z = z_ref[...]
    y = jnp.concatenate([onsa_ref[...], ohg_ref[...], o_pool, omem_ref[...]], axis=1) * (z * _sigmoid(z))
    y_ref[...] = x_ref[...] + _hdot(y, wo_ref[...])


def _sfinal_call(x, onsa, ohg, u, buf, omem, z, pw, ps, wo, *, layer):
    full = lambda a: pl.BlockSpec(a.shape, lambda i: (0, 0))
    buf_spec = pl.BlockSpec((1,) + buf.shape[1:], lambda i: (layer, 0, 0, 0))
    return pl.pallas_call(
        _sfinal_kernel,
        grid=(1,),
        in_specs=[full(x), full(onsa), full(ohg), full(u), buf_spec, full(omem), full(z), full(pw), full(ps), full(wo)],
        out_specs=full(x),
        out_shape=jax.ShapeDtypeStruct(x.shape, F32),
        compiler_params=_cparams(("arbitrary",)),
    )(x, onsa, ohg, u, buf, omem, z, pw, ps, wo)


def _block_diag_ones(w):
    i = np.arange(w)
    return jnp.asarray((i[:, None] // HEAD_DIM) == (i[None, :] // HEAD_DIM), BF16)


def _rope_tables(pos):
    half = HEAD_DIM // 2
    inv = ROPE_THETA ** (-jnp.arange(half, dtype=jnp.float32) / half)
    ang = pos.astype(jnp.float32)[:, None] * inv[None, :]
    cos = jnp.cos(ang)
    sin = jnp.sin(ang)
    cos64 = jnp.concatenate([cos, cos], axis=1)
    sin64 = jnp.concatenate([-sin, sin], axis=1)
    one = jnp.ones_like(cos64)
    zero = jnp.zeros_like(sin64)
    cq = jnp.tile(cos64, (1, N_HEADS))
    sq = jnp.tile(sin64, (1, N_HEADS))
    ckv = jnp.concatenate([cos64, one] * 3, axis=1)
    skv = jnp.concatenate([sin64, zero] * 3, axis=1)
    return cq, sq, ckv, skv


def _layer_params(l, norm_g, w_in, w_out, nsa_qn, nsa_kn, cmp_pe, cmp_w1, cmp_w2, lbs, hg_on, pool_w, pool_scale,
                  mem_norm, w_mem_kv, mem_qn, mem_kn):
    w = w_in[l]
    w_pad = jnp.concatenate([w[:, :652], jnp.zeros((D_MODEL, 116), F32), w[:, 652:]], axis=1)
    ones64 = jnp.ones((HEAD_DIM,), F32)
    kn = jnp.concatenate([nsa_kn[l, 0], ones64, nsa_kn[l, 1], ones64, nsa_kn[l, 2], ones64])[None, :]
    w1 = cmp_w1[l].reshape(2, L_CMP, HEAD_DIM, CMP_HID)
    w1e = jnp.zeros((L_CMP, 4, HEAD_DIM, 2 * CMP_HID), F32)
    w1e = w1e.at[:, 0, :, :CMP_HID].set(w1[0]).at[:, 1, :, CMP_HID:].set(w1[1])
    w2e = jnp.zeros((2 * CMP_HID, 128), F32)
    w2e = w2e.at[:CMP_HID, :HEAD_DIM].set(cmp_w2[l, 0]).at[CMP_HID:, HEAD_DIM:].set(cmp_w2[l, 1])
    pe = jnp.zeros((L_CMP, 4, HEAD_DIM), F32).at[:, 0].set(cmp_pe[l, 0]).at[:, 1].set(cmp_pe[l, 1])
    wr = jnp.zeros((L_CMP, 2, HEAD_DIM, 2 * CMP_HID), F32)
    wr = wr.at[:, 0, :, :CMP_HID].set(w1[0]).at[:, 1, :, CMP_HID:].set(w1[1])
    pw = jnp.zeros((GROUP_W, GROUP_W), F32)
    for g in range(len(POOL_SIZES)):
        pw = pw.at[g * 64:(g + 1) * 64, g * 64:(g + 1) * 64].set(pool_w[l, g])
    lb = lbs[l][None, :]
    return dict(
        g=norm_g[l][None, :], w32=w_pad, w16=w_pad.astype(BF16), wo32=w_out[l], wo16=w_out[l].astype(BF16),
        qn=jnp.tile(nsa_qn[l], N_HEADS)[None, :], kn=kn, mqn=jnp.tile(mem_qn[l], N_HEADS)[None, :],
        pe=pe.reshape(1, L_CMP * 256), w1e=w1e.reshape(L_CMP * 256, 2 * CMP_HID).astype(BF16), w2e=w2e.astype(BF16),
        pe2=jnp.tile(jnp.concatenate([cmp_pe[l, 0].T, cmp_pe[l, 1].T], axis=0), (1, 2 * PAGE_SIZE // L_CMP)),
        wr2=wr.reshape(L_CMP // 2, 256, 2 * CMP_HID).astype(BF16),
        la=jnp.log(lb), lc=jnp.log1p(-lb), oml=1.0 - lb, on=hg_on[l][None, :],
        pw32=pw, pw16=pw.astype(BF16), ps=pool_scale[l][None, :],
        mg=mem_norm[l][None, :], wm16=w_mem_kv[l].astype(BF16), mkn=jnp.tile(mem_kn[l], N_HEADS)[None, :],
    )


def _prompt_layer(x, mem, p, tabs, consts):
    b, t, _ = x.shape
    bd256, bd384, mk, _ = consts
    outs = _proj_call(x.reshape(b * t, D_MODEL), p['g'], p['w16'], *tabs, p['qn'], p['kn'], p['mqn'], bd256, bd384,
                      tm=512, t_tiles=t // 512, precise=False)
    q, nsa, win, gates, hq, hf, hi, u, qm, z = [o.reshape(b, t, -1) for o in outs]
    n_blk = t // L_CMP
    cc = _cmp_call(nsa.reshape(b * n_blk, L_CMP * 256), p['pe'], p['w1e'], p['w2e'],
                   tr=min(256, b * n_blk), n_rows=b * n_blk, row_off=0)
    cc = jnp.pad(cc.reshape(b, n_blk, 128), ((0, 0), (0, 128 - n_blk), (0, 0)))
    o_nsa = _nsa_call(q, nsa, win, gates, cc, tq=NSA_TQ)
    o_hg, st = _hgrn_call(hq, hf, hi, p['la'], p['lc'], p['oml'], p['on'], mk, bd256)
    mkv = _memkv_call(mem, p['mg'], p['wm16'], p['mkn'], bd256)
    y = _final_call(x, o_nsa, o_hg, u, qm, z, mkv, p['pw16'], p['ps'], p['wo16'], tm=min(512, t))
    st5 = st.reshape(b, N_HEADS, HEAD_DIM, N_HEADS, HEAD_DIM)
    s_new = jnp.stack([st5[:, h, :, h, :] for h in range(N_HEADS)], axis=1)
    s_new = jnp.swapaxes(s_new, 2, 3)
    wb = min(WINDOW, t)
    return (y, nsa.reshape(b, t, 4, 1, HEAD_DIM), win[:, t - wb:].reshape(b, wb, 2, 1, HEAD_DIM), s_new,
            u[:, t - POOL_BUF:], mkv.reshape(b, N_MEM, 2, N_HEADS, HEAD_DIM))


def _sample_layer(l, x, p, tabs, consts, cache_pages, cache_pairs, win_cache, state, pool_buf, cmem, page_table,
                  n_phys):
    nb = x.shape[0]
    bd256, bd384, _, perm = consts
    outs = _proj_call(x, p['g'], p['w32'], *tabs, p['qn'], p['kn'], p['mqn'], bd256, bd384,
                      tm=nb, t_tiles=1, precise=True)
    q, nsa, win, gates, hq, hf, hi, u, qm, z = outs
    npg = page_table.shape[1]
    cc = _cmp_pages_call(page_table, cache_pages, p['pe2'], perm, p['wr2'], p['w2e'], pp=min(64, npg),
                         page_off=l * n_phys)
    idx, o_c = _scmp_call(q, cc.reshape(nb * npg, 512), nb=nb, npg=npg)
    r3 = lambda a: a.reshape(nb, 1, -1)
    o_nsa = _ssel_call(page_table, idx, cache_pairs, q, nsa, win_cache, win, gates, o_c, layer=l, n_phys=n_phys)
    o_hg, o_mem, s_new = _srest_call(r3(hq), r3(hf), r3(hi), r3(qm), state, cmem,
                                     p['la'], p['lc'], p['oml'], p['on'], layer=l)
    y = _sfinal_call(x, o_nsa.reshape(nb, 256), o_hg.reshape(nb, 256), u, pool_buf, o_mem.reshape(nb, 256), z,
                     p['pw32'], p['ps'], p['wo32'], layer=l)
    return y, nsa, win, s_new.reshape(nb, N_HEADS, HEAD_DIM, HEAD_DIM), u


def kernel(x_prompt, x_sample, mem_prompt, cache_nsa, cache_nsa_win, state_hgrn, state_pool, cache_mem,
           page_table, norm_g, w_in, w_out, nsa_qn, nsa_kn, cmp_pe, cmp_w1, cmp_w2, hg_lb, hg_on,
           pool_w, pool_scale, mem_norm, w_mem_kv, mem_qn, mem_kn):
    depth = w_in.shape[0]
    b, t, _ = x_prompt.shape
    nb = x_sample.shape[0]
    past_len = page_table.shape[1] * PAGE_SIZE
    n_phys = cache_nsa.shape[1]
    wb = cache_nsa_win.shape[2]

    lbs = jnp.cumsum(jax.nn.softmax(hg_lb.astype(jnp.float32), axis=0), axis=0)
    lbs = lbs - lbs[0:1]
    consts = (_block_diag_ones(256), _block_diag_ones(384), jnp.asarray(_hgrn_consts(), F32), _pair_perm())
    tabs_p = _rope_tables(jnp.arange(t))
    tabs_s = tuple(jnp.broadcast_to(a, (nb, a.shape[1])) for a in _rope_tables(past_len + jnp.arange(1)))

    cache_t = jnp.transpose(cache_nsa, (0, 1, 3, 4, 5, 2))
    cache_pages = cache_t.reshape(depth * n_phys, 4 * HEAD_DIM, PAGE_SIZE)
    cache_pairs = cache_t.reshape(depth * n_phys * 2, 2 * HEAD_DIM, PAGE_SIZE)
    win_cache = jnp.transpose(cache_nsa_win, (0, 1, 3, 4, 5, 2)).reshape(depth * nb, 2 * HEAD_DIM, wb)
    state = state_hgrn.reshape(depth * nb, GROUP_W, HEAD_DIM)
    pool_buf = jnp.transpose(state_pool, (0, 2, 1, 3))
    cmem = jnp.transpose(cache_mem, (0, 1, 3, 4, 5, 2)).reshape(depth * nb, 2 * GROUP_W, N_MEM)

    xp, xs = x_prompt, x_sample.reshape(nb, D_MODEL)
    acc = [[] for _ in range(9)]
    for l in range(depth):
        p = _layer_params(l, norm_g, w_in, w_out, nsa_qn, nsa_kn, cmp_pe, cmp_w1, cmp_w2, lbs, hg_on, pool_w,
                          pool_scale, mem_norm, w_mem_kv, mem_qn, mem_kn)
        xp, a, bw, c, d, e = _prompt_layer(xp, mem_prompt, p, tabs_p, consts)
        xs, sa, sw, sc, sd = _sample_layer(l, xs, p, tabs_s, consts, cache_pages, cache_pairs, win_cache, state,
                                           pool_buf, cmem, page_table, n_phys)
        new_win_s = jnp.concatenate([cache_nsa_win[l][:, 1:], sw.reshape(nb, 1, 2, 1, HEAD_DIM)], axis=1)[:, -wb:]
        new_pool_s = jnp.concatenate([state_pool[l][:, 1:], sd.reshape(nb, 1, GROUP_W)], axis=1)
        for lst, val in zip(acc, (a, bw, c, d, e, sa.reshape(nb, 1, 4, 1, HEAD_DIM), new_win_s, sc, new_pool_s)):
            lst.append(val)
    return (xp, xs.reshape(nb, 1, D_MODEL)) + tuple(jnp.stack(v) for v in acc)
```

```python
import functools

import numpy as np
import jax
import jax.numpy as jnp
from jax import lax
from jax.experimental import pallas as pl
from jax.experimental.pallas import tpu as pltpu

D_MODEL = 1024
HEAD_DIM = 64
GROUP_W = 256
N_HEADS = 4
L_CMP = 32
L_SLC = 64
N_SEL = 16
WINDOW = 512
CMP_HID = 256
POOL_SIZES = (2, 4, 8, 16)
POOL_BUF = 15
N_MEM = 256
PAGE_SIZE = 128
ROPE_THETA = 10000.0
EPS = 1e-6
QK_SCALE = HEAD_DIM ** -0.5

PROJ_W = 3072
HG_CHUNK = 256
NSA_TQ = 256
NEG = -1e30
VMEM_LIMIT = 56 * 1024 * 1024

F32 = jnp.float32
BF16 = jnp.bfloat16
HI = lax.Precision.HIGHEST


def _cparams(sem):
    return pltpu.CompilerParams(dimension_semantics=sem, vmem_limit_bytes=VMEM_LIMIT)


def _bdot(a, b):
    return jnp.dot(a.astype(BF16), b.astype(BF16), preferred_element_type=F32)


def _bdot_nt(a, b):
    return lax.dot_general(a.astype(BF16), b.astype(BF16), (((1,), (1,)), ((), ())), preferred_element_type=F32)


def _hdot(a, b):
    return jnp.dot(a, b, precision=HI, preferred_element_type=F32)


def _hdot_nt(a, b):
    return lax.dot_general(a, b, (((1,), (1,)), ((), ())), precision=HI, preferred_element_type=F32)


def _dot2(a, b16):
    hi = a.astype(BF16)
    lo = (a - hi.astype(F32)).astype(BF16)
    return jnp.dot(hi, b16, preferred_element_type=F32) + jnp.dot(lo, b16, preferred_element_type=F32)


def _dot2_left(a16, b):
    hi = b.astype(BF16)
    lo = (b - hi.astype(F32)).astype(BF16)
    return jnp.dot(a16, hi, preferred_element_type=F32) + jnp.dot(a16, lo, preferred_element_type=F32)


def _seg_rms(x, bd16, gain):
    ms = _dot2(x * x, bd16) * (1.0 / HEAD_DIM)
    return x * lax.rsqrt(ms + EPS) * gain


def _rope(x, cos, sin_signed):
    w = x.shape[1]
    lane = lax.broadcasted_iota(jnp.int32, x.shape, 1)
    first = (lane & 32) == 0
    rot = jnp.where(first, pltpu.roll(x, w - 32, 1), pltpu.roll(x, 32, 1))
    return x * cos + rot * sin_signed


def _sigmoid(x):
    return jax.nn.sigmoid(x)


def _lane_head(shape):
    return lax.broadcasted_iota(jnp.int32, shape, 1) // HEAD_DIM


def _proj_kernel(x_ref, g_ref, w_ref, cq_ref, sq_ref, ckv_ref, skv_ref, qn_ref, kn_ref, mqn_ref, bd256_ref, bd384_ref,
                 q_out, nsa_out, win_out, gate_out, hq_out, hf_out, hi_out, u_out, qm_out, z_out, *, precise):
    x = x_ref[...]
    ms = jnp.mean(x * x, axis=-1, keepdims=True)
    h = x * lax.rsqrt(ms + EPS) * g_ref[...]
    proj = _hdot(h, w_ref[...]) if precise else _bdot(h, w_ref[...])
    bd256 = bd256_ref[...]
    q = _rope(_seg_rms(proj[:, 0:256], bd256, qn_ref[...]), cq_ref[...], sq_ref[...])
    q_out[...] = q
    kv = proj[:, 256:640]
    kvn = _rope(_seg_rms(kv, bd384_ref[...], kn_ref[...]), ckv_ref[...], skv_ref[...])
    lane = lax.broadcasted_iota(jnp.int32, kv.shape, 1)
    is_key = ((lane // HEAD_DIM) & 1) == 0
    kv = jnp.where(is_key, kvn, kv)
    nsa_out[...] = kv[:, 0:256]
    win_out[...] = kv[:, 256:384]
    gate_out[...] = _sigmoid(proj[:, 640:768])
    hq_out[...] = proj[:, 768:1024]
    hf_out[...] = proj[:, 1024:1280]
    hi_out[...] = proj[:, 1280:1536]
    u_out[...] = proj[:, 1536:1792]
    qm_out[...] = _seg_rms(proj[:, 1792:2048], bd256, mqn_ref[...])
    z_out[...] = proj[:, 2048:3072]


def _proj_call(x, g, w, cq, sq, ckv, skv, qn, kn, mqn, bd256, bd384, *, tm, t_tiles, precise):
    n = x.shape[0]
    row = lambda wd: pl.BlockSpec((tm, wd), lambda i: (i, 0))
    tab = lambda wd: pl.BlockSpec((tm, wd), lambda i: (i % t_tiles, 0))
    full = lambda a: pl.BlockSpec(a.shape, lambda i: (0, 0))
    widths = (256, 256, 128, 128, 256, 256, 256, 256, 256, 1024)
    return pl.pallas_call(
        functools.partial(_proj_kernel, precise=precise),
        grid=(n // tm,),
        in_specs=[row(D_MODEL), full(g), full(w), tab(256), tab(256), tab(384), tab(384),
                  full(qn), full(kn), full(mqn), full(bd256), full(bd384)],
        out_specs=[row(wd) for wd in widths],
        out_shape=[jax.ShapeDtypeStruct((n, wd), F32) for wd in widths],
        compiler_params=_cparams(("parallel",)),
    )(x, g, w, cq, sq, ckv, skv, qn, kn, mqn, bd256, bd384)


def _cmp_kernel(x_ref, pe_ref, w1_ref, w2_ref, o_ref):
    x = x_ref[...] + pe_ref[...]
    h = _bdot(x, w1_ref[...])
    h = h * _sigmoid(h)
    o_ref[...] = _bdot(h, w2_ref[...])


def _cmp_call(x, pe, w1, w2, *, tr, n_rows, row_off):
    off = row_off // tr
    full = lambda a: pl.BlockSpec(a.shape, lambda i: (0, 0))
    return pl.pallas_call(
        _cmp_kernel,
        grid=(n_rows // tr,),
        in_specs=[pl.BlockSpec((tr, x.shape[1]), lambda i: (i + off, 0)), full(pe), full(w1), full(w2)],
        out_specs=pl.BlockSpec((tr, 128), lambda i: (i, 0)),
        out_shape=jax.ShapeDtypeStruct((n_rows, 128), F32),
        compiler_params=_cparams(("parallel",)),
    )(x, pe, w1, w2)


def _cmp_pages_kernel(pt_ref, pages_hbm, pe_ref, perm_ref, wr_ref, w2_ref, o_ref, pg_buf, xs_scr, sem,
                      *, pp, nb, npg, pg_off):
    per_seq = npg // pp
    n_out = pp * (PAGE_SIZE // L_CMP)
    pe2 = pe_ref[...]
    perm = perm_ref[...]

    def copies(step, slot):
        b = step // per_seq
        i0 = (step - b * per_seq) * pp
        return [pltpu.make_async_copy(pages_hbm.at[pg_off + pt_ref[b, i0 + j], pl.ds(0, 2 * HEAD_DIM), :],
                                      pg_buf.at[slot, j], sem.at[slot]) for j in range(pp)]

    for cp in copies(0, 0):
        cp.start()

    def trip(step, carry):
        slot = step % 2

        @pl.when(step + 1 < nb * per_seq)
        def _():
            for cp in copies(step + 1, 1 - slot):
                cp.start()

        for cp in copies(step, slot):
            cp.wait()
        for i in range(pp // 2):
            m2 = jnp.concatenate([pg_buf[slot, 2 * i], pg_buf[slot, 2 * i + 1]], axis=1) + pe2
            xp = _bdot(m2, perm)
            xs_scr[:, 8 * i:8 * i + 8, :] = xp.T.reshape(L_CMP, 8, 128)
        acc = jnp.zeros((n_out, 2 * CMP_HID), F32)
        for r2 in range(L_CMP // 2):
            slab = jnp.concatenate([xs_scr[2 * r2], xs_scr[2 * r2 + 1]], axis=1)
            acc = acc + _bdot(slab, wr_ref[r2])
        h = acc * _sigmoid(acc)
        o_ref[pl.ds(pl.multiple_of(step * n_out, n_out), n_out), :] = _bdot(h, w2_ref[...])
        return carry

    lax.fori_loop(0, nb * per_seq, trip, 0)


def _cmp_pages_call(page_table, pages, pe2, perm, wr2, w2, *, pp, page_off):
    nb, npg = page_table.shape
    n_out = pp * (PAGE_SIZE // L_CMP)
    full = lambda a: pl.BlockSpec(a.shape, lambda i, pt: (0,) * a.ndim)
    n_rows = nb * npg * (PAGE_SIZE // L_CMP)
    return pl.pallas_call(
        functools.partial(_cmp_pages_kernel, pp=pp, nb=nb, npg=npg, pg_off=page_off),
        grid_spec=pltpu.PrefetchScalarGridSpec(
            num_scalar_prefetch=1,
            grid=(1,),
            in_specs=[pl.BlockSpec(memory_space=pl.ANY), full(pe2), full(perm), full(wr2), full(w2)],
            out_specs=pl.BlockSpec((n_rows, 128), lambda i, pt: (0, 0)),
            scratch_shapes=[pltpu.VMEM((2, pp, 2 * HEAD_DIM, PAGE_SIZE), F32), pltpu.VMEM((L_CMP, n_out, 128), F32),
                            pltpu.SemaphoreType.DMA((2,))],
        ),
        out_shape=jax.ShapeDtypeStruct((n_rows, 128), F32),
        compiler_params=_cparams(("arbitrary",)),
    )(page_table, pages, pe2, perm, wr2, w2)


def _pair_perm():
    a = np.zeros((2, PAGE_SIZE, L_CMP, 8), np.float32)
    for r in range(L_CMP):
        for pq in range(2):
            for nb in range(PAGE_SIZE // L_CMP):
                a[pq, L_CMP * nb + r, r, 4 * pq + nb] = 1.0
    return jnp.asarray(a.reshape(2 * PAGE_SIZE, L_CMP * 8), BF16)


def _topk_mask(score, k, idx_f, axis=1):
    sel = jnp.zeros(score.shape, jnp.bool_)
    s = score
    big = jnp.float32(1e9)
    picks = []
    for _ in range(k):
        m = jnp.max(s, axis=axis, keepdims=True)
        cand = jnp.where(s == m, idx_f, big)
        pick = jnp.min(cand, axis=axis, keepdims=True)
        hit = idx_f == pick
        sel = sel | hit
        s = jnp.where(hit, -jnp.inf, s)
        picks.append(pick)
    return sel, picks


def _flash_step(carry, k16, vt16, bias, qs_scr):
    return _flash_update(carry, _scores(k16, qs_scr), vt16, bias)


def _scores(k16, qs_scr):
    return lax.dot_general(k16, qs_scr[...], (((1,), (1,)), ((), ())), preferred_element_type=F32)


def _flash_update(carry, s, vt16, bias):
    m, l, acc = carry
    if bias is not None:
        s = s + (jnp.concatenate([bias] * N_HEADS, axis=1) if bias.ndim == 2 else bias)
    m_new = jnp.maximum(m, jnp.max(s, axis=0, keepdims=True))
    alpha = jnp.exp(m - m_new)
    p = jnp.exp(s - m_new)
    l = alpha * l + jnp.sum(p, axis=0, keepdims=True)
    acc = alpha * acc + jnp.dot(vt16, p.astype(BF16), preferred_element_type=F32)
    return m_new, l, acc


def _nsa_kernel(q_ref, nsa_ref, win_ref, gate_ref, cc_ref, o_ref, qs_scr, vt_scr, wt_scr, *, tq):
    i = pl.program_id(1)
    t_len = nsa_ref.shape[1]
    assert WINDOW == 2 * tq and t_len % tq == 0
    n_cmp = HEAD_DIM

    @pl.when(i == 0)
    def _():
        for c in range(t_len // tq):
            sl = slice(c * tq, (c + 1) * tq)
            vt_scr[c] = nsa_ref[0, sl, 128:256].T.astype(BF16)
            wt_scr[c] = win_ref[0, sl, :].T.astype(BF16)

    t0 = i * tq
    r4 = N_HEADS * tq
    q = q_ref[0] * QK_SCALE
    for h in range(N_HEADS):
        qs_scr[h * tq:(h + 1) * tq, :] = q[:, h * HEAD_DIM:(h + 1) * HEAD_DIM].astype(BF16)
    t_pos = t0 + lax.broadcasted_iota(jnp.int32, (1, tq), 1)
    t_pos4 = t0 + lax.broadcasted_iota(jnp.int32, (1, r4), 1) % tq

    cc = cc_ref[0]
    kcc16 = cc[0:n_cmp, 0:HEAD_DIM].astype(BF16)
    vcct16 = cc.T[HEAD_DIM:2 * HEAD_DIM, 0:n_cmp].astype(BF16)
    done = (lax.broadcasted_iota(jnp.int32, (n_cmp, r4), 0) + 1) * L_CMP - 1 <= t_pos4
    s = lax.dot_general(kcc16, qs_scr[...], (((1,), (1,)), ((), ())), preferred_element_type=F32)
    s = jnp.where(done, s, NEG)
    m = jnp.max(s, axis=0, keepdims=True)
    m = jnp.where(m < 0.5 * NEG, 0.0, m)
    p = jnp.where(done, jnp.exp(s - m), 0.0)
    p = p / jnp.maximum(jnp.sum(p, axis=0, keepdims=True), 1e-30)
    o_c = jnp.dot(vcct16, p.astype(BF16), preferred_element_type=F32)
    imp = p[:, 0:tq] + p[:, tq:2 * tq] + p[:, 2 * tq:3 * tq] + p[:, 3 * tq:4 * tq]

    n_idx = lax.broadcasted_iota(jnp.int32, (n_cmp, tq), 0)
    imp2 = imp + pltpu.roll(imp, n_cmp - 1, 0)
    blk_t = t_pos // L_SLC
    j = n_idx // 2
    valid = ((n_idx & 1) == 0) & (j < t_len // L_SLC)
    avail = valid & (j <= blk_t)
    forced = (j == 0) | (j == blk_t) | (j == blk_t - 1)
    score = jnp.where(avail, jnp.where(forced, jnp.inf, imp2), -jnp.inf)
    sel, _ = _topk_mask(score, N_SEL, n_idx.astype(F32), axis=0)
    sel16 = jnp.where(sel & avail, 1.0, 0.0).astype(BF16)

    key_l = lax.broadcasted_iota(jnp.int32, (tq, tq), 0)
    qry_l = lax.broadcasted_iota(jnp.int32, (tq, tq), 1)
    causal = jnp.where(key_l <= qry_l, 0.0, NEG)
    tail = jnp.where(key_l > qry_l, 0.0, NEG)

    def sel_bias(c):
        ek = c * tq + lax.broadcasted_iota(jnp.int32, (tq, n_cmp), 0)
        en = lax.broadcasted_iota(jnp.int32, (tq, n_cmp), 1)
        expand = jnp.where(en == 2 * (ek // L_SLC), 1.0, 0.0).astype(BF16)
        msel = jnp.dot(expand, sel16, preferred_element_type=F32)
        return (msel - 1.0) * (-NEG)

    def chunk(ref, vt, c, lo):
        start = pl.multiple_of(c * tq, tq)
        return ref[0, pl.ds(start, tq), lo:lo + HEAD_DIM].astype(BF16), vt[c, HEAD_DIM:2 * HEAD_DIM, :]

    init = (jnp.full((1, r4), NEG, F32), jnp.zeros((1, r4), F32), jnp.zeros((HEAD_DIM, r4), F32))

    def sel_body(c, carry):
        k16, vt16 = chunk(nsa_ref, vt_scr, c, 128)
        return _flash_step(carry, k16, vt16, sel_bias(c), qs_scr)

    st_s = lax.fori_loop(0, i, sel_body, init)
    k16, vt16 = chunk(nsa_ref, vt_scr, i, 128)
    st_s = _flash_step(st_s, k16, vt16, sel_bias(i) + causal, qs_scr)

    k16, vt16 = chunk(win_ref, wt_scr, i, 0)
    st_w = _flash_step(init, k16, vt16, causal, qs_scr)
    k16, vt16 = chunk(win_ref, wt_scr, jnp.maximum(i - 1, 0), 0)
    st_w = _flash_step(st_w, k16, vt16, jnp.where(i >= 1, 0.0, NEG), qs_scr)
    k16, vt16 = chunk(win_ref, wt_scr, jnp.maximum(i - 2, 0), 0)
    st_w = _flash_step(st_w, k16, vt16, tail + jnp.where(i >= 2, 0.0, NEG), qs_scr)

    gt = gate_ref[0].T
    g_c, g_s, g_w = (jnp.concatenate([gt[3 * h + br:3 * h + br + 1] for h in range(N_HEADS)], axis=1)
                     for br in range(3))
    o_s = st_s[2] / jnp.maximum(st_s[1], 1e-30)
    o_w = st_w[2] / jnp.maximum(st_w[1], 1e-30)
    o = g_c * o_c + g_s * o_s + g_w * o_w
    o_ref[0] = jnp.concatenate([o[:, h * tq:(h + 1) * tq] for h in range(N_HEADS)], axis=0).T


def _nsa_call(q, nsa, win, gates, cc, *, tq):
    b, t, _ = q.shape
    n_ch = t // tq
    return pl.pallas_call(
        functools.partial(_nsa_kernel, tq=tq),
        grid=(b, n_ch),
        in_specs=[pl.BlockSpec((1, tq, 256), lambda bi, i: (bi, i, 0)),
                  pl.BlockSpec((1, t, 256), lambda bi, i: (bi, 0, 0)),
                  pl.BlockSpec((1, t, 128), lambda bi, i: (bi, 0, 0)),
                  pl.BlockSpec((1, tq, 128), lambda bi, i: (bi, i, 0)),
                  pl.BlockSpec((1, 128, 128), lambda bi, i: (bi, 0, 0))],
        out_specs=pl.BlockSpec((1, tq, 256), lambda bi, i: (bi, i, 0)),
        out_shape=jax.ShapeDtypeStruct((b, t, 256), F32),
        scratch_shapes=[pltpu.VMEM((N_HEADS * tq, HEAD_DIM), BF16), pltpu.VMEM((n_ch, 128, tq), BF16),
                        pltpu.VMEM((n_ch, 128, tq), BF16)],
        compiler_params=_cparams(("parallel", "arbitrary")),
    )(q, nsa, win, gates, cc)


def _hgrn_consts(c=HG_CHUNK):
    t = np.arange(c)[:, None]
    k = np.arange(c)[None, :]
    masks = []
    m = 1
    while m < c:
        upper = (t % (2 * m)) >= m
        masks.append(upper & ((k // (2 * m)) == t // (2 * m)) & ((k % (2 * m)) < m))
        m *= 2
    masks.append(t == k)
    return np.stack(masks).astype(np.float32)


def _block_row_bcast(p, m):
    c, w = p.shape
    if 2 * m >= 8:
        p3 = p.reshape(c // (2 * m), 2 * m, w)
        return jnp.broadcast_to(p3[:, m - 1:m, :], p3.shape).reshape(c, w)
    off = lax.broadcasted_iota(jnp.int32, p.shape, 0) % (2 * m)
    out = p
    for o in range(2 * m):
        if o != m - 1:
            out = jnp.where(off == o, pltpu.roll(p, (o - (m - 1)) % c, 0), out)
    return out


def _log_decay(fr, la, lc):
    ls = jnp.minimum(fr, 0.0) - jnp.log(1.0 + jnp.exp(-jnp.abs(fr)))
    b = lc + ls
    return jnp.maximum(la, b) + jnp.log(1.0 + jnp.exp(-jnp.abs(la - b)))


def _hgrn_kernel(hq_ref, hf_ref, hi_ref, la_ref, lc_ref, oml_ref, on_ref, mk_ref, bd_ref,
                 o_ref, st_ref, st_scr):
    c = HG_CHUNK
    jt = pl.program_id(1)

    @pl.when(jt == 0)
    def _():
        st_scr[...] = jnp.zeros_like(st_scr)

    q = hq_ref[0]
    fr = hf_ref[0]
    v = hi_ref[0]
    logf = _log_decay(fr, la_ref[...], lc_ref[...])
    k = oml_ref[...] * _sigmoid(-fr)
    row = lax.broadcasted_iota(jnp.int32, (c, GROUP_W), 0)
    pre = logf
    decays = []
    m = 1
    while m < c:
        mid = _block_row_bcast(pre, m)
        upper = (row % (2 * m)) >= m
        decays.append(jnp.exp(jnp.where(upper, pre, mid - pre)))
        pre = pre + jnp.where(upper, mid, 0.0)
        m *= 2
    gc = pre
    gl = gc[c - 1:c, :]
    lane_h = _lane_head((c, GROUP_W))
    n_lvl = mk_ref.shape[0]
    a_tot = [jnp.zeros((c, c), F32) for _ in range(N_HEADS)]
    for lv in range(n_lvl):
        qt = q * decays[lv] if lv < len(decays) else q
        kt = (k * decays[lv] if lv < len(decays) else k).astype(BF16)
        qs = jnp.concatenate([jnp.where(lane_h == h, qt, 0.0).astype(BF16) for h in range(N_HEADS)], axis=0)
        a_l = lax.dot_general(qs, kt, (((1,), (1,)), ((), ())), preferred_element_type=F32)
        mk = mk_ref[lv]
        for h in range(N_HEADS):
            a_tot[h] = a_tot[h] + mk * a_l[h * c:(h + 1) * c]
    a_all = jnp.concatenate(a_tot, axis=0)
    r = _bdot(a_all, v)
    o = jnp.zeros((c, GROUP_W), F32)
    for h in range(N_HEADS):
        o = o + jnp.where(lane_h == h, r[h * c:(h + 1) * c], 0.0)
    st = st_scr[...]
    o = o + _bdot_nt(q * jnp.exp(gc), st)
    khat = k * jnp.exp(gl - gc)
    upd = jnp.dot(v.T.astype(BF16), khat.astype(BF16), preferred_element_type=F32)
    row_h = lax.broadcasted_iota(jnp.int32, (GROUP_W, GROUP_W), 0) // HEAD_DIM
    col_h = _lane_head((GROUP_W, GROUP_W))
    st_new = jnp.exp(gl) * st + jnp.where(row_h == col_h, upd, 0.0)
    st_scr[...] = st_new
    st_ref[0] = st_new
    o_ref[0] = _seg_rms(o, bd_ref[...], on_ref[...])


def _hgrn_call(hq, hf, hi, la, lc, oml, on, mk, bd256):
    b, t, _ = hq.shape
    c = HG_CHUNK
    row = pl.BlockSpec((1, c, GROUP_W), lambda bi, j: (bi, j, 0))
    full2 = lambda a: pl.BlockSpec(a.shape, lambda bi, j: (0, 0))
    return pl.pallas_call(
        _hgrn_kernel,
        grid=(b, t // c),
        in_specs=[row, row, row, full2(la), full2(lc), full2(oml), full2(on),
                  pl.BlockSpec(mk.shape, lambda bi, j: (0, 0, 0)), full2(bd256)],
        out_specs=[row, pl.BlockSpec((1, GROUP_W, GROUP_W), lambda bi, j: (bi, 0, 0))],
        out_shape=[jax.ShapeDtypeStruct((b, t, GROUP_W), F32), jax.ShapeDtypeStruct((b, GROUP_W, GROUP_W), F32)],
        scratch_shapes=[pltpu.VMEM((GROUP_W, GROUP_W), F32)],
        compiler_params=_cparams(("parallel", "arbitrary")),
    )(hq, hf, hi, la, lc, oml, on, mk, bd256)


def _memkv_kernel(m_ref, g_ref, w_ref, kn_ref, bd_ref, o_ref):
    x = m_ref[0]
    ms = jnp.mean(x * x, axis=-1, keepdims=True)
    h = x * lax.rsqrt(ms + EPS) * g_ref[...]
    kv = _bdot(h, w_ref[...])
    o_ref[0, :, 0:GROUP_W] = _seg_rms(kv[:, 0:GROUP_W], bd_ref[...], kn_ref[...])
    o_ref[0, :, GROUP_W:2 * GROUP_W] = kv[:, GROUP_W:2 * GROUP_W]


def _memkv_call(mem, g, w, kn, bd256):
    b = mem.shape[0]
    full = lambda a: pl.BlockSpec(a.shape, lambda i: (0, 0))
    return pl.pallas_call(
        _memkv_kernel,
        grid=(b,),
        in_specs=[pl.BlockSpec((1, N_MEM, D_MODEL), lambda i: (i, 0, 0)), full(g), full(w), full(kn), full(bd256)],
        out_specs=pl.BlockSpec((1, N_MEM, 2 * GROUP_W), lambda i: (i, 0, 0)),
        out_shape=jax.ShapeDtypeStruct((b, N_MEM, 2 * GROUP_W), F32),
        compiler_params=_cparams(("parallel",)),
    )(mem, g, w, kn, bd256)


def _final_kernel(x_ref, onsa_ref, ohg_ref, u_ref, halo_ref, qm_ref, z_ref, mkv_ref, pw_ref, ps_ref, wo_ref,
                  y_ref, ext_scr, *, tm):
    jt = pl.program_id(1)
    u = u_ref[0]
    halo = halo_ref[0]
    ext_scr[0:16, :] = jnp.where(jt == 0, 0.0, halo)
    ext_scr[16:16 + tm, :] = u
    pos = jt * tm + lax.broadcasted_iota(jnp.int32, (tm, 1), 0)
    lane_g = _lane_head((tm, GROUP_W))
    pooled = jnp.zeros((tm, GROUP_W), F32)
    w = 1
    for gi, wsz in enumerate(POOL_SIZES):
        assert wsz == 2 * w
        ext_scr[w:16 + tm, :] = ext_scr[w:16 + tm, :] + ext_scr[0:16 + tm - w, :]
        w = wsz
        cnt = jnp.minimum(wsz, pos + 1).astype(F32)
        pooled = jnp.where(lane_g == gi, ext_scr[16:16 + tm, :] / cnt, pooled)
    o_pool = _bdot(pooled - u, pw_ref[...]) * ps_ref[...]

    qm = qm_ref[0] * QK_SCALE
    mkv = mkv_ref[0]
    heads = []
    for h in range(N_HEADS):
        sl = slice(h * HEAD_DIM, (h + 1) * HEAD_DIM)
        s = _bdot_nt(qm[:, sl], mkv[:, sl])
        s = s - jnp.max(s, axis=1, keepdims=True)
        p = jnp.exp(s)
        p = p / jnp.sum(p, axis=1, keepdims=True)
        heads.append(_bdot(p, mkv[:, GROUP_W + h * HEAD_DIM:GROUP_W + (h + 1) * HEAD_DIM]))
    o_mem = jnp.concatenate(heads, axis=1)

    z = z_ref[0]
    y = jnp.concatenate([onsa_ref[0], ohg_ref[0], o_pool, o_mem], axis=1) * (z * _sigmoid(z))
    y_ref[0] = x_ref[0] + _bdot(y, wo_ref[...])


def _final_call(x, onsa, ohg, u, qm, z, mkv, pw, ps, wo, *, tm):
    b, t, _ = x.shape
    row = lambda wd: pl.BlockSpec((1, tm, wd), lambda bi, j: (bi, j, 0))
    full2 = lambda a: pl.BlockSpec(a.shape, lambda bi, j: (0, 0))
    hb = tm // 16
    return pl.pallas_call(
        functools.partial(_final_kernel, tm=tm),
        grid=(b, t // tm),
        in_specs=[row(D_MODEL), row(GROUP_W), row(GROUP_W), row(GROUP_W),
                  pl.BlockSpec((1, 16, GROUP_W), lambda bi, j: (bi, jnp.maximum(j * hb - 1, 0), 0)),
                  row(GROUP_W), row(D_MODEL),
                  pl.BlockSpec((1, N_MEM, 2 * GROUP_W), lambda bi, j: (bi, 0, 0)),
                  full2(pw), full2(ps), full2(wo)],
        out_specs=row(D_MODEL),
        out_shape=jax.ShapeDtypeStruct((b, t, D_MODEL), F32),
        scratch_shapes=[pltpu.VMEM((tm + 16, GROUP_W), F32)],
        compiler_params=_cparams(("parallel", "arbitrary")),
    )(x, onsa, ohg, u, u, qm, z, mkv, pw, ps, wo)


def _head_rows(row):
    r8 = lax.broadcasted_iota(jnp.int32, (8, GROUP_W), 0)
    qm = jnp.where(r8 == _lane_head((8, GROUP_W)), jnp.broadcast_to(row, (8, GROUP_W)), 0.0)
    return qm[:, 0:64] + qm[:, 64:128] + qm[:, 128:192] + qm[:, 192:256]


def _scmp_kernel(q_ref, cc_ref, idx_ref, oc_ref, a_scr, b_scr, *, nb, npg):
    b = pl.program_id(0)
    gb = cc_ref[...]
    qh = _head_rows(q_ref[pl.ds(b, 1), :]) * QK_SCALE
    ss = [_hdot_nt(qh, gb[:, r * 128:r * 128 + HEAD_DIM]) for r in range(4)]
    m = ss[0].max(axis=1, keepdims=True)
    for r in range(1, 4):
        m = jnp.maximum(m, ss[r].max(axis=1, keepdims=True))
    ps = [jnp.exp(s - m) for s in ss]
    l = ps[0].sum(axis=1, keepdims=True)
    for r in range(1, 4):
        l = l + ps[r].sum(axis=1, keepdims=True)
    inv = 1.0 / jnp.maximum(l, 1e-30)
    ps = [p * inv for p in ps]
    oc = _hdot(ps[0], gb[:, HEAD_DIM:128])
    for r in range(1, 4):
        oc = oc + _hdot(ps[r], gb[:, r * 128 + HEAD_DIM:(r + 1) * 128])
    oc_ref[pl.ds(pl.multiple_of(b * 8, 8), 8), :] = oc
    hmask = lax.broadcasted_iota(jnp.int32, ss[0].shape, 0) < N_HEADS
    imps = [jnp.sum(jnp.where(hmask, p, 0.0), axis=0, keepdims=True) for p in ps]
    a_scr[pl.ds(b, 1), :] = imps[0] + imps[1]
    b_scr[pl.ds(b, 1), :] = imps[2] + imps[3]

    @pl.when(b == nb - 1)
    def _():
        score = jnp.concatenate([a_scr[...], b_scr[...]], axis=1)
        lane = lax.broadcasted_iota(jnp.int32, score.shape, 1)
        jidx = jnp.where(lane < npg, 2 * lane, 2 * (lane - npg) + 1)
        forced = (jidx == 0) | (jidx == 2 * npg - 1)
        score = jnp.where(forced, jnp.inf, score)
        _, picks = _topk_mask(score, N_SEL - 1, jidx.astype(F32))
        out_lane = lax.broadcasted_iota(jnp.int32, (nb, 128), 1)
        out = jnp.zeros((nb, 128), jnp.int32)
        for r, pk in enumerate(picks):
            out = jnp.where(out_lane == r, pk.astype(jnp.int32), out)
        idx_ref[...] = out


def _scmp_call(q, cc_rows, *, nb, npg):
    return pl.pallas_call(
        functools.partial(_scmp_kernel, nb=nb, npg=npg),
        grid=(nb,),
        in_specs=[pl.BlockSpec(q.shape, lambda b: (0, 0)), pl.BlockSpec((npg, 512), lambda b: (b, 0))],
        out_specs=[pl.BlockSpec((nb, 128), lambda b: (0, 0)), pl.BlockSpec((nb * 8, HEAD_DIM), lambda b: (0, 0))],
        out_shape=[jax.ShapeDtypeStruct((nb, 128), jnp.int32), jax.ShapeDtypeStruct((nb * 8, HEAD_DIM), F32)],
        scratch_shapes=[pltpu.VMEM((nb, 128), F32), pltpu.VMEM((nb, 128), F32)],
        compiler_params=_cparams(("arbitrary",)),
    )(q, cc_rows)


def _ssel_kernel(pt_ref, idx_ref, pairs_hbm, wc_hbm, q_ref, new_ref, wnew_ref, gate_ref, oc_ref, o_ref,
                 blk_buf, wc_buf, blk_sem, wc_sem, *, n_gather, nb, pg_off, wc_off):
    def copies(b, slot):
        out = []
        for r in range(n_gather):
            src = (pg_off + pt_ref[b, idx_ref[b, r] // 2]) * 2 + 1
            out.append(pltpu.make_async_copy(pairs_hbm.at[src], blk_buf.at[slot, r], blk_sem.at[slot]))
        out.append(pltpu.make_async_copy(wc_hbm.at[wc_off + b], wc_buf.at[slot], wc_sem.at[slot]))
        return out

    for cp in copies(0, 0):
        cp.start()
    lane_half = lax.broadcasted_iota(jnp.int32, (8, PAGE_SIZE), 1) // L_SLC

    def per_seq(b, carry):
        slot = b % 2

        @pl.when(b + 1 < nb)
        def _():
            for cp in copies(b + 1, 1 - slot):
                cp.start()

        for cp in copies(b, slot):
            cp.wait()
        qh = _head_rows(q_ref[pl.ds(b, 1), :]) * QK_SCALE
        new = new_ref[pl.ds(b, 1), :]
        s_n = jnp.sum(qh * new[:, 128:192], axis=1, keepdims=True)
        scores = []
        m = s_n
        for r in range(n_gather):
            half = idx_ref[b, r] % 2
            s = _hdot(qh, blk_buf[slot, r, 0:HEAD_DIM, :])
            s = jnp.where(lane_half == half, s, NEG)
            scores.append(s)
            m = jnp.maximum(m, s.max(axis=1, keepdims=True))
        p_n = jnp.exp(s_n - m)
        l = p_n
        acc = p_n * new[:, 192:256]
        for r in range(n_gather):
            p = jnp.exp(scores[r] - m)
            l = l + p.sum(axis=1, keepdims=True)
            acc = acc + _hdot_nt(p, blk_buf[slot, r, HEAD_DIM:2 * HEAD_DIM, :])
        o_s = acc / jnp.maximum(l, 1e-30)
        wnew = wnew_ref[pl.ds(b, 1), :]
        s_w = _hdot(qh, wc_buf[slot, 0:HEAD_DIM, :])
        keep = lax.broadcasted_iota(jnp.int32, s_w.shape, 1) >= s_w.shape[1] + 1 - WINDOW
        s_w = jnp.where(keep, s_w, NEG)
        s_wn = jnp.sum(qh * wnew[:, 0:HEAD_DIM], axis=1, keepdims=True)
        m_w = jnp.maximum(s_w.max(axis=1, keepdims=True), s_wn)
        p_w = jnp.where(keep, jnp.exp(s_w - m_w), 0.0)
        p_wn = jnp.exp(s_wn - m_w)
        l_w = p_w.sum(axis=1, keepdims=True) + p_wn
        o_w = (_hdot_nt(p_w, wc_buf[slot, HEAD_DIM:128, :]) + p_wn * wnew[:, HEAD_DIM:128]) / jnp.maximum(l_w, 1e-30)
        o_c = oc_ref[pl.ds(pl.multiple_of(b * 8, 8), 8), :]
        g = gate_ref[pl.ds(b, 1), :]
        outs = []
        for h in range(N_HEADS):
            outs.append(g[:, 3 * h:3 * h + 1] * o_c[h:h + 1] + g[:, 3 * h + 1:3 * h + 2] * o_s[h:h + 1]
                        + g[:, 3 * h + 2:3 * h + 3] * o_w[h:h + 1])
        o_ref[pl.ds(b, 1), :] = jnp.concatenate(outs, axis=1)
        return carry

    lax.fori_loop(0, nb, per_seq, 0)


def _ssel_call(page_table, idx, cache_pairs, q, new_rows, win_cache, win_new, gates, o_c, *, layer, n_phys):
    nb = q.shape[0]
    n_gather = N_SEL - 1
    wb = win_cache.shape[2]
    full = lambda a: pl.BlockSpec(a.shape, lambda i, pt, ix: (0, 0))
    hbm = pl.BlockSpec(memory_space=pl.ANY)
    return pl.pallas_call(
        functools.partial(_ssel_kernel, n_gather=n_gather, nb=nb, pg_off=layer * n_phys, wc_off=layer * nb),
        grid_spec=pltpu.PrefetchScalarGridSpec(
            num_scalar_prefetch=2,
            grid=(1,),
            in_specs=[hbm, hbm, full(q), full(new_rows), full(win_new), full(gates), full(o_c)],
            out_specs=pl.BlockSpec((nb, 256), lambda i, pt, ix: (0, 0)),
            scratch_shapes=[pltpu.VMEM((2, n_gather, 128, PAGE_SIZE), F32), pltpu.VMEM((2, 128, wb), F32),
                            pltpu.SemaphoreType.DMA((2,)), pltpu.SemaphoreType.DMA((2,))],
        ),
        out_shape=jax.ShapeDtypeStruct((nb, 256), F32),
        compiler_params=_cparams(("arbitrary",)),
    )(page_table, idx, cache_pairs, win_cache, q, new_rows, win_new, gates, o_c)


def _to_col(row, eye):
    return jnp.sum(jnp.where(eye, jnp.broadcast_to(row, eye.shape), 0.0), axis=1, keepdims=True)


def _srest_kernel(hq_ref, hf_ref, hi_ref, qm_ref, st_ref, cm_ref, la_ref, lc_ref, oml_ref, on_ref,
                  ohg_ref, omem_ref, stn_ref):
    q = hq_ref[0]
    fr = hf_ref[0]
    v = hi_ref[0]
    logf = _log_decay(fr, la_ref[...], lc_ref[...])
    f = jnp.exp(logf)
    k = oml_ref[...] * _sigmoid(-fr)
    eye = (lax.broadcasted_iota(jnp.int32, (GROUP_W, GROUP_W), 0)
           == lax.broadcasted_iota(jnp.int32, (GROUP_W, GROUP_W), 1))
    qc, fc, kc = _to_col(q, eye), _to_col(f, eye), _to_col(k, eye)
    vt = jnp.concatenate([jnp.broadcast_to(v[:, h * HEAD_DIM:(h + 1) * HEAD_DIM], (HEAD_DIM, HEAD_DIM))
                          for h in range(N_HEADS)], axis=0)
    s_new = fc * st_ref[0] + kc * vt
    stn_ref[0] = s_new
    x = qc * s_new
    on = on_ref[...]
    outs = []
    for h in range(N_HEADS):
        oh = jnp.sum(x[h * HEAD_DIM:(h + 1) * HEAD_DIM], axis=0, keepdims=True)
        ms = jnp.mean(oh * oh, axis=1, keepdims=True)
        outs.append(oh * lax.rsqrt(ms + EPS) * on[:, h * HEAD_DIM:(h + 1) * HEAD_DIM])
    ohg_ref[0] = jnp.concatenate(outs, axis=1)

    cm = cm_ref[0]
    qrow = qm_ref[0] * QK_SCALE
    r8 = lax.broadcasted_iota(jnp.int32, (8, GROUP_W), 0)
    lh8 = _lane_head((8, GROUP_W))
    qbd = jnp.where(r8 == lh8, jnp.broadcast_to(qrow, (8, GROUP_W)), 0.0)
    s = _hdot(qbd, cm[0:GROUP_W])
    s = s - s.max(axis=1, keepdims=True)
    p = jnp.exp(s)
    p = p / p.sum(axis=1, keepdims=True)
    o8 = _hdot_nt(p, cm[GROUP_W:2 * GROUP_W])
    omem_ref[0] = jnp.sum(jnp.where(r8 == lh8, o8, 0.0), axis=0, keepdims=True)


def _srest_call(hq, hf, hi, qm, state, cmem, la, lc, oml, on, *, layer):
    nb = hq.shape[0]
    per_b = lambda rows, wd: pl.BlockSpec((1, rows, wd), lambda b: (b, 0, 0))
    per_lb = lambda rows, wd: pl.BlockSpec((1, rows, wd), lambda b: (layer * nb + b, 0, 0))
    full = lambda a: pl.BlockSpec(a.shape, lambda b: (0, 0))
    return pl.pallas_call(
        _srest_kernel,
        grid=(nb,),
        in_specs=[per_b(1, 256)] * 4 + [per_lb(GROUP_W, HEAD_DIM), per_lb(2 * GROUP_W, N_MEM),
                                        full(la), full(lc), full(oml), full(on)],
        out_specs=[per_b(1, 256), per_b(1, 256), per_b(GROUP_W, HEAD_DIM)],
        out_shape=[jax.ShapeDtypeStruct((nb, 1, 256), F32)] * 2 + [jax.ShapeDtypeStruct((nb, GROUP_W, HEAD_DIM), F32)],
        compiler_params=_cparams(("parallel",)),
    )(hq, hf, hi, qm, state, cmem, la, lc, oml, on)


def _sfinal_kernel(x_ref, onsa_ref, ohg_ref, u_ref, buf_ref, omem_ref, z_ref, pw_ref, ps_ref, wo_ref, y_ref):
    u = u_ref[...]
    lane_g = _lane_head(u.shape)
    run = u
    pooled = jnp.zeros(u.shape, F32)
    nxt = 1
    for gi, wsz in enumerate(POOL_SIZES):
        for sh in range(nxt, wsz):
            run = run + buf_ref[0, POOL_BUF - sh]
        nxt = wsz
        pooled = jnp.where(lane_g == gi, run / float(wsz), pooled)
    o_pool = _hdot(pooled - u, pw_ref[...]) * ps_ref[...]
    z = z_ref[...]
    y = jnp.concatenate([onsa_ref[...], ohg_ref[...], o_pool, omem_ref[...]], axis=1) * (z * _sigmoid(z))
    y_ref[...] = x_ref[...] + _hdot(y, wo_ref[...])


def _sfinal_call(x, onsa, ohg, u, buf, omem, z, pw, ps, wo, *, layer):
    full = lambda a: pl.BlockSpec(a.shape, lambda i: (0, 0))
    buf_spec = pl.BlockSpec((1,) + buf.shape[1:], lambda i: (layer, 0, 0, 0))
    return pl.pallas_call(
        _sfinal_kernel,
        grid=(1,),
        in_specs=[full(x), full(onsa), full(ohg), full(u), buf_spec, full(omem), full(z), full(pw), full(ps), full(wo)],
        out_specs=full(x),
        out_shape=jax.ShapeDtypeStruct(x.shape, F32),
        compiler_params=_cparams(("arbitrary",)),
    )(x, onsa, ohg, u, buf, omem, z, pw, ps, wo)


def _block_diag_ones(w):
    i = np.arange(w)
    return jnp.asarray((i[:, None] // HEAD_DIM) == (i[None, :] // HEAD_DIM), BF16)


def _rope_tables(pos):
    half = HEAD_DIM // 2
    inv = ROPE_THETA ** (-jnp.arange(half, dtype=jnp.float32) / half)
    ang = pos.astype(jnp.float32)[:, None] * inv[None, :]
    cos = jnp.cos(ang)
    sin = jnp.sin(ang)
    cos64 = jnp.concatenate([cos, cos], axis=1)
    sin64 = jnp.concatenate([-sin, sin], axis=1)
    one = jnp.ones_like(cos64)
    zero = jnp.zeros_like(sin64)
    cq = jnp.tile(cos64, (1, N_HEADS))
    sq = jnp.tile(sin64, (1, N_HEADS))
    ckv = jnp.concatenate([cos64, one] * 3, axis=1)
    skv = jnp.concatenate([sin64, zero] * 3, axis=1)
    return cq, sq, ckv, skv


def _layer_params(l, norm_g, w_in, w_out, nsa_qn, nsa_kn, cmp_pe, cmp_w1, cmp_w2, lbs, hg_on, pool_w, pool_scale,
                  mem_norm, w_mem_kv, mem_qn, mem_kn):
    w = w_in[l]
    w_pad = jnp.concatenate([w[:, :652], jnp.zeros((D_MODEL, 116), F32), w[:, 652:]], axis=1)
    ones64 = jnp.ones((HEAD_DIM,), F32)
    kn = jnp.concatenate([nsa_kn[l, 0], ones64, nsa_kn[l, 1], ones64, nsa_kn[l, 2], ones64])[None, :]
    w1 = cmp_w1[l].reshape(2, L_CMP, HEAD_DIM, CMP_HID)
    w1e = jnp.zeros((L_CMP, 4, HEAD_DIM, 2 * CMP_HID), F32)
    w1e = w1e.at[:, 0, :, :CMP_HID].set(w1[0]).at[:, 1, :, CMP_HID:].set(w1[1])
    w2e = jnp.zeros((2 * CMP_HID, 128), F32)
    w2e = w2e.at[:CMP_HID, :HEAD_DIM].set(cmp_w2[l, 0]).at[CMP_HID:, HEAD_DIM:].set(cmp_w2[l, 1])
    pe = jnp.zeros((L_CMP, 4, HEAD_DIM), F32).at[:, 0].set(cmp_pe[l, 0]).at[:, 1].set(cmp_pe[l, 1])
    wr = jnp.zeros((L_CMP, 2, HEAD_DIM, 2 * CMP_HID), F32)
    wr = wr.at[:, 0, :, :CMP_HID].set(w1[0]).at[:, 1, :, CMP_HID:].set(w1[1])
    pw = jnp.zeros((GROUP_W, GROUP_W), F32)
    for g in range(len(POOL_SIZES)):
        pw = pw.at[g * 64:(g + 1) * 64, g * 64:(g + 1) * 64].set(pool_w[l, g])
    lb = lbs[l][None, :]
    return dict(
        g=norm_g[l][None, :], w32=w_pad, w16=w_pad.astype(BF16), wo32=w_out[l], wo16=w_out[l].astype(BF16),
        qn=jnp.tile(nsa_qn[l], N_HEADS)[None, :], kn=kn, mqn=jnp.tile(mem_qn[l], N_HEADS)[None, :],
        pe=pe.reshape(1, L_CMP * 256), w1e=w1e.reshape(L_CMP * 256, 2 * CMP_HID).astype(BF16), w2e=w2e.astype(BF16),
        pe2=jnp.tile(jnp.concatenate([cmp_pe[l, 0].T, cmp_pe[l, 1].T], axis=0), (1, 2 * PAGE_SIZE // L_CMP)),
        wr2=wr.reshape(L_CMP // 2, 256, 2 * CMP_HID).astype(BF16),
        la=jnp.log(lb), lc=jnp.log1p(-lb), oml=1.0 - lb, on=hg_on[l][None, :],
        pw32=pw, pw16=pw.astype(BF16), ps=pool_scale[l][None, :],
        mg=mem_norm[l][None, :], wm16=w_mem_kv[l].astype(BF16), mkn=jnp.tile(mem_kn[l], N_HEADS)[None, :],
    )


def _prompt_layer(x, mem, p, tabs, consts):
    b, t, _ = x.shape
    bd256, bd384, mk, _ = consts
    outs = _proj_call(x.reshape(b * t, D_MODEL), p['g'], p['w16'], *tabs, p['qn'], p['kn'], p['mqn'], bd256, bd384,
                      tm=512, t_tiles=t // 512, precise=False)
    q, nsa, win, gates, hq, hf, hi, u, qm, z = [o.reshape(b, t, -1) for o in outs]
    n_blk = t // L_CMP
    cc = _cmp_call(nsa.reshape(b * n_blk, L_CMP * 256), p['pe'], p['w1e'], p['w2e'],
                   tr=min(256, b * n_blk), n_rows=b * n_blk, row_off=0)
    cc = jnp.pad(cc.reshape(b, n_blk, 128), ((0, 0), (0, 128 - n_blk), (0, 0)))
    o_nsa = _nsa_call(q, nsa, win, gates, cc, tq=NSA_TQ)
    o_hg, st = _hgrn_call(hq, hf, hi, p['la'], p['lc'], p['oml'], p['on'], mk, bd256)
    mkv = _memkv_call(mem, p['mg'], p['wm16'], p['mkn'], bd256)
    y = _final_call(x, o_nsa, o_hg, u, qm, z, mkv, p['pw16'], p['ps'], p['wo16'], tm=min(512, t))
    st5 = st.reshape(b, N_HEADS, HEAD_DIM, N_HEADS, HEAD_DIM)
    s_new = jnp.stack([st5[:, h, :, h, :] for h in range(N_HEADS)], axis=1)
    s_new = jnp.swapaxes(s_new, 2, 3)
    wb = min(WINDOW, t)
    return (y, nsa.reshape(b, t, 4, 1, HEAD_DIM), win[:, t - wb:].reshape(b, wb, 2, 1, HEAD_DIM), s_new,
            u[:, t - POOL_BUF:], mkv.reshape(b, N_MEM, 2, N_HEADS, HEAD_DIM))


def _sample_layer(l, x, p, tabs, consts, cache_pages, cache_pairs, win_cache, state, pool_buf, cmem, page_table,
                  n_phys):
    nb = x.shape[0]
    bd256, bd384, _, perm = consts
    outs = _proj_call(x, p['g'], p['w32'], *tabs, p['qn'], p['kn'], p['mqn'], bd256, bd384,
                      tm=nb, t_tiles=1, precise=True)
    q, nsa, win, gates, hq, hf, hi, u, qm, z = outs
    npg = page_table.shape[1]
    cc = _cmp_pages_call(page_table, cache_pages, p['pe2'], perm, p['wr2'], p['w2e'], pp=min(64, npg),
                         page_off=l * n_phys)
    idx, o_c = _scmp_call(q, cc.reshape(nb * npg, 512), nb=nb, npg=npg)
    r3 = lambda a: a.reshape(nb, 1, -1)
    o_nsa = _ssel_call(page_table, idx, cache_pairs, q, nsa, win_cache, win, gates, o_c, layer=l, n_phys=n_phys)
    o_hg, o_mem, s_new = _srest_call(r3(hq), r3(hf), r3(hi), r3(qm), state, cmem,
                                     p['la'], p['lc'], p['oml'], p['on'], layer=l)
    y = _sfinal_call(x, o_nsa.reshape(nb, 256), o_hg.reshape(nb, 256), u, pool_buf, o_mem.reshape(nb, 256), z,
                     p['pw32'], p['ps'], p['wo32'], layer=l)
    return y, nsa, win, s_new.reshape(nb, N_HEADS, HEAD_DIM, HEAD_DIM), u


def kernel(x_prompt, x_sample, mem_prompt, cache_nsa, cache_nsa_win, state_hgrn, state_pool, cache_mem,
           page_table, norm_g, w_in, w_out, nsa_qn, nsa_kn, cmp_pe, cmp_w1, cmp_w2, hg_lb, hg_on,
           pool_w, pool_scale, mem_norm, w_mem_kv, mem_qn, mem_kn):
    depth = w_in.shape[0]
    b, t, _ = x_prompt.shape
    nb = x_sample.shape[0]
    past_len = page_table.shape[1] * PAGE_SIZE
    n_phys = cache_nsa.shape[1]
    wb = cache_nsa_win.shape[2]

    lbs = jnp.cumsum(jax.nn.softmax(hg_lb.astype(jnp.float32), axis=0), axis=0)
    lbs = lbs - lbs[0:1]
    consts = (_block_diag_ones(256), _block_diag_ones(384), jnp.asarray(_hgrn_consts(), F32), _pair_perm())
    tabs_p = _rope_tables(jnp.arange(t))
    tabs_s = tuple(jnp.broadcast_to(a, (nb, a.shape[1])) for a in _rope_tables(past_len + jnp.arange(1)))

    cache_t = jnp.transpose(cache_nsa, (0, 1, 3, 4, 5, 2))
    cache_pages = cache_t.reshape(depth * n_phys, 4 * HEAD_DIM, PAGE_SIZE)
    cache_pairs = cache_t.reshape(depth * n_phys * 2, 2 * HEAD_DIM, PAGE_SIZE)
    win_cache = jnp.transpose(cache_nsa_win, (0, 1, 3, 4, 5, 2)).reshape(depth * nb, 2 * HEAD_DIM, wb)
    state = state_hgrn.reshape(depth * nb, GROUP_W, HEAD_DIM)
    pool_buf = jnp.transpose(state_pool, (0, 2, 1, 3))
    cmem = jnp.transpose(cache_mem, (0, 1, 3, 4, 5, 2)).reshape(depth * nb, 2 * GROUP_W, N_MEM)

    xp, xs = x_prompt, x_sample.reshape(nb, D_MODEL)
    acc = [[] for _ in range(9)]
    for l in range(depth):
        p = _layer_params(l, norm_g, w_in, w_out, nsa_qn, nsa_kn, cmp_pe, cmp_w1, cmp_w2, lbs, hg_on, pool_w,
                          pool_scale, mem_norm, w_mem_kv, mem_qn, mem_kn)
        xp, a, bw, c, d, e = _prompt_layer(xp, mem_prompt, p, tabs_p, consts)
        xs, sa, sw, sc, sd = _sample_layer(l, xs, p, tabs_s, consts, cache_pages, cache_pairs, win_cache, state,
                                           pool_buf, cmem, page_table, n_phys)
        new_win_s = jnp.concatenate([cache_nsa_win[l][:, 1:], sw.reshape(nb, 1, 2, 1, HEAD_DIM)], axis=1)[:, -wb:]
        new_pool_s = jnp.concatenate([state_pool[l][:, 1:], sd.reshape(nb, 1, GROUP_W)], axis=1)
        for lst, val in zip(acc, (a, bw, c, d, e, sa.reshape(nb, 1, 4, 1, HEAD_DIM), new_win_s, sc, new_pool_s)):
            lst.append(val)
    return (xp, xs.reshape(nb, 1, D_MODEL)) + tuple(jnp.stack(v) for v in acc)
```
